```python
import math
import jax, jax.numpy as jnp
from jax import lax
import numpy as np

D_MODEL = 2048
BATCH = 4
SEQ = 2048
DEPTH = 4
DEC_BATCH = 32
DEC_SEQ = 32
PAST_LEN = 4096

CHUNK = 64
D_SSM = 1024
SSM_GROUP = 16
N_GROUPS = D_SSM // SSM_GROUP
STATE_DIM = 64
N_HEADS = 4
HEAD_DIM = 128
V_DIM = 2 * HEAD_DIM
D_QK = N_HEADS * 2 * HEAD_DIM
D_ATTN = N_HEADS * V_DIM
N_BRANCH = 2
D_IN = D_SSM + 2 * D_QK + D_ATTN + N_BRANCH * D_MODEL
D_FF = (8 * D_MODEL + 3 * 256 - 1) // (3 * 256) * 256
Q_BLOCK = 128
EPS = 1e-6

kernel_name = "hybrid_s5_diffattn_streaming_step"


def rmsnorm(x, g):
    xf = x.astype(jnp.float32)
    y = xf * lax.rsqrt(jnp.mean(xf * xf, axis=-1, keepdims=True) + EPS)
    return (y * g.astype(jnp.float32)).astype(x.dtype)


def alibi_slopes():
    return jnp.exp2(-8.0 * jnp.arange(1, N_HEADS + 1, dtype=jnp.float32) / N_HEADS)


def lambda_init(layer):
    return 0.8 - 0.6 * math.exp(-0.3 * layer)


def diff_lambda(lq1, lk1, lq2, lk2, lam_init):
    f = jnp.float32
    return (jnp.exp(jnp.sum(lq1.astype(f) * lk1.astype(f)))
            - jnp.exp(jnp.sum(lq2.astype(f) * lk2.astype(f))) + lam_init)


def position_bias(qpos, kpos):
    dist = jnp.abs(qpos[:, None] - kpos[None, :]).astype(jnp.float32)
    allowed = (kpos[None, :] // CHUNK) <= (qpos[:, None] // CHUNK)
    bias = -alibi_slopes()[:, None, None, None] * dist
    return jnp.where(allowed, bias, -jnp.inf)


def diff_weights(logits, lam):
    p = jax.nn.softmax(logits, axis=-1)
    return p[..., 0, :, :] - lam * p[..., 1, :, :]


def attn_prompt(q, k, v, lam):
    B, L = q.shape[0], q.shape[1]
    n_blk = L // Q_BLOCK
    kpos = jnp.arange(L)
    scale = HEAD_DIM ** -0.5

    def block(i):
        start = i * Q_BLOCK
        qb = lax.dynamic_slice_in_dim(q, start, Q_BLOCK, axis=1)
        qpos = start + jnp.arange(Q_BLOCK)
        s = jnp.einsum('bqhmd,bkhmd->bhmqk', qb, k).astype(jnp.float32) * scale
        w = diff_weights(s + position_bias(qpos, kpos), lam)
        return jnp.einsum('bhqk,bkhe->bqhe', w.astype(v.dtype), v)

    out = lax.map(block, jnp.arange(n_blk))
    return jnp.moveaxis(out, 0, 1).reshape(B, L, N_HEADS, V_DIM)


def attn_sample(q, k_new, v_new, k_past, v_past, lam):
    S = q.shape[1]
    P = k_past.shape[1]
    scale = HEAD_DIM ** -0.5
    s_past = jnp.einsum('bqhmd,bkhmd->bhmqk', q, k_past)
    s_new = jnp.einsum('bqhmd,bkhmd->bhmqk', q, k_new)
    s = jnp.concatenate([s_past, s_new], axis=-1).astype(jnp.float32) * scale
    qpos = P + jnp.arange(S)
    kpos = jnp.arange(P + S)
    w = diff_weights(s + position_bias(qpos, kpos), lam).astype(v_new.dtype)
    return (jnp.einsum('bhqk,bkhe->bqhe', w[..., :P], v_past)
            + jnp.einsum('bhqk,bkhe->bqhe', w[..., P:], v_new))


def ssm_discretise(a_re, a_im, log_dt, b_re, b_im):
    f = jnp.float32
    a_re = a_re.astype(f); a_im = a_im.astype(f)
    b_re = b_re.astype(f); b_im = b_im.astype(f)
    dt = jnp.exp(log_dt.astype(f))[:, None]
    mag = jnp.exp(dt * a_re)
    ab_re = mag * jnp.cos(dt * a_im)
    ab_im = mag * jnp.sin(dt * a_im)
    n_re = ab_re - 1.0
    n_im = ab_im
    den = a_re * a_re + a_im * a_im
    c_re = (n_re * a_re + n_im * a_im) / den
    c_im = (n_im * a_re - n_re * a_im) / den
    bb_re = c_re[..., None] * b_re - c_im[..., None] * b_im
    bb_im = c_re[..., None] * b_im + c_im[..., None] * b_re
    return ab_re, ab_im, bb_re, bb_im


def ssm_combine(e1, e2):
    a1r, a1i, b1r, b1i = e1
    a2r, a2i, b2r, b2i = e2
    return (a2r * a1r - a2i * a1i,
            a2r * a1i + a2i * a1r,
            a2r * b1r - a2i * b1i + b2r,
            a2r * b1i + a2i * b1r + b2i)


def ssm_branch(u, prm, h0_re, h0_im):
    Bn, L = u.shape[0], u.shape[1]
    uf = u.astype(jnp.float32)
    ug = uf.reshape(Bn, L, N_GROUPS, SSM_GROUP)
    ab_re, ab_im, bb_re, bb_im = ssm_discretise(prm['ssm_a_re'], prm['ssm_a_im'], prm['ssm_log_dt'],
                                                prm['ssm_b_re'], prm['ssm_b_im'])
    bu_re = jnp.einsum('blgc,gpc->blgp', ug, bb_re)
    bu_im = jnp.einsum('blgc,gpc->blgp', ug, bb_im)
    if h0_re is not None:
        hr = h0_re.astype(jnp.float32)
        hi = h0_im.astype(jnp.float32)
        bu_re = bu_re.at[:, 0].add(ab_re * hr - ab_im * hi)
        bu_im = bu_im.at[:, 0].add(ab_re * hi + ab_im * hr)
    ar = jnp.broadcast_to(ab_re, bu_re.shape)
    ai = jnp.broadcast_to(ab_im, bu_im.shape)
    _, _, h_re, h_im = lax.associative_scan(ssm_combine, (ar, ai, bu_re, bu_im), axis=1)
    c_re = prm['ssm_c_re'].astype(jnp.float32)
    c_im = prm['ssm_c_im'].astype(jnp.float32)
    y = jnp.einsum('blgp,gcp->blgc', h_re, c_re) - jnp.einsum('blgp,gcp->blgc', h_im, c_im)
    y = y.reshape(Bn, L, D_SSM) + prm['ssm_d'].astype(jnp.float32) * uf
    z = jax.nn.gelu(y)
    out = z * jax.nn.sigmoid(z @ prm['w_glu'].astype(jnp.float32) + prm['b_glu'].astype(jnp.float32))
    return out.astype(u.dtype), h_re[:, -1].astype(u.dtype), h_im[:, -1].astype(u.dtype)


def layer_forward(x, layer, prm, h0_re, h0_im, k_past, v_past):
    Bn, L = x.shape[0], x.shape[1]
    h = rmsnorm(x, prm['g_pre_mix'])
    proj = h @ prm['w_in']
    u_ssm, q, k, v, gl = jnp.split(
        proj, [D_SSM, D_SSM + D_QK, D_SSM + 2 * D_QK, D_SSM + 2 * D_QK + D_ATTN], axis=-1)
    q = q.reshape(Bn, L, N_HEADS, 2, HEAD_DIM)
    k = k.reshape(Bn, L, N_HEADS, 2, HEAD_DIM)
    v = v.reshape(Bn, L, N_HEADS, V_DIM)

    a_out, hT_re, hT_im = ssm_branch(u_ssm, prm, h0_re, h0_im)

    lam_init = lambda_init(layer)
    lam = diff_lambda(prm['lam_q1'], prm['lam_k1'], prm['lam_q2'], prm['lam_k2'], lam_init)
    if k_past is None:
        o = attn_prompt(q, k, v, lam)
    else:
        o = attn_sample(q, k, v, k_past, v_past, lam)
    o = (rmsnorm(o, prm['g_sub']) * (1.0 - lam_init)).reshape(Bn, L, D_ATTN)

    gates = jax.nn.sigmoid(gl.reshape(Bn, L, N_BRANCH, D_MODEL) + prm['b_gate'])
    merged = gates[..., 0, :] * (a_out @ prm['w_br_ssm']) + gates[..., 1, :] * (o @ prm['w_br_attn'])
    x = x + rmsnorm(merged @ prm['w_out'], prm['g_post_mix'])

    hf = rmsnorm(x, prm['g_pre_ffn'])
    f = (jax.nn.silu(hf @ prm['w_ffn_gate']) * (hf @ prm['w_ffn_up'])) @ prm['w_ffn_down']
    x = x + rmsnorm(f, prm['g_post_ffn'])
    return x, k, v, hT_re, hT_im


def setup_inputs(seed: int = 0) -> dict:
    key = jax.random.key(seed)
    ks = iter(jax.random.split(key, 40))

    def nrm(shape, scale):
        return jax.random.normal(next(ks), shape, jnp.float32) * scale

    def gain(shape):
        return 1.0 + nrm(shape, 0.01)

    n_idx = jnp.arange(STATE_DIM, dtype=jnp.float32)
    inp = {}
    inp['x_prompt'] = nrm((BATCH, SEQ, D_MODEL), 1.0)
    inp['x_sample'] = nrm((DEC_BATCH, DEC_SEQ, D_MODEL), 1.0)
    inp['cache_k'] = nrm((DEPTH, DEC_BATCH, PAST_LEN, N_HEADS, 2, HEAD_DIM), 1.0)
    inp['cache_v'] = nrm((DEPTH, DEC_BATCH, PAST_LEN, N_HEADS, V_DIM), 1.0)
    inp['state_ssm_re'] = nrm((DEPTH, DEC_BATCH, N_GROUPS, STATE_DIM), 0.1)
    inp['state_ssm_im'] = nrm((DEPTH, DEC_BATCH, N_GROUPS, STATE_DIM), 0.1)
    inp['g_pre_mix'] = gain((DEPTH, D_MODEL))
    inp['w_in'] = nrm((DEPTH, D_MODEL, D_IN), D_MODEL ** -0.5)
    inp['b_gate'] = nrm((DEPTH, N_BRANCH, D_MODEL), 0.02)
    inp['ssm_a_re'] = -0.5 + nrm((DEPTH, N_GROUPS, STATE_DIM), 0.01)
    inp['ssm_a_im'] = math.pi * n_idx + nrm((DEPTH, N_GROUPS, STATE_DIM), 0.01)
    inp['ssm_log_dt'] = jax.random.uniform(next(ks), (DEPTH, N_GROUPS), jnp.float32,
                                           minval=math.log(1e-3), maxval=math.log(1e-1))
    inp['ssm_b_re'] = nrm((DEPTH, N_GROUPS, STATE_DIM, SSM_GROUP), (2 * SSM_GROUP) ** -0.5)
    inp['ssm_b_im'] = nrm((DEPTH, N_GROUPS, STATE_DIM, SSM_GROUP), (2 * SSM_GROUP) ** -0.5)
    inp['ssm_c_re'] = nrm((DEPTH, N_GROUPS, SSM_GROUP, STATE_DIM), (2 * STATE_DIM) ** -0.5)
    inp['ssm_c_im'] = nrm((DEPTH, N_GROUPS, SSM_GROUP, STATE_DIM), (2 * STATE_DIM) ** -0.5)
    inp['ssm_d'] = nrm((DEPTH, D_SSM), 1.0)
    inp['w_glu'] = nrm((DEPTH, D_SSM, D_SSM), D_SSM ** -0.5)
    inp['b_glu'] = nrm((DEPTH, D_SSM), 0.02)
    inp['lam_q1'] = nrm((DEPTH, HEAD_DIM), 0.1)
    inp['lam_k1'] = nrm((DEPTH, HEAD_DIM), 0.1)
    inp['lam_q2'] = nrm((DEPTH, HEAD_DIM), 0.1)
    inp['lam_k2'] = nrm((DEPTH, HEAD_DIM), 0.1)
    inp['g_sub'] = gain((DEPTH, V_DIM))
    inp['w_br_ssm'] = nrm((DEPTH, D_SSM, D_MODEL), D_SSM ** -0.5)
    inp['w_br_attn'] = nrm((DEPTH, D_ATTN, D_MODEL), D_ATTN ** -0.5)
    inp['w_out'] = nrm((DEPTH, D_MODEL, D_MODEL), D_MODEL ** -0.5)
    inp['g_post_mix'] = gain((DEPTH, D_MODEL))
    inp['g_pre_ffn'] = gain((DEPTH, D_MODEL))
    inp['w_ffn_gate'] = nrm((DEPTH, D_MODEL, D_FF), D_MODEL ** -0.5)
    inp['w_ffn_up'] = nrm((DEPTH, D_MODEL, D_FF), D_MODEL ** -0.5)
    inp['w_ffn_down'] = nrm((DEPTH, D_FF, D_MODEL), D_FF ** -0.5)
    inp['g_post_ffn'] = gain((DEPTH, D_MODEL))
    return inp


def reference(x_prompt, x_sample, cache_k, cache_v, state_ssm_re, state_ssm_im,
              g_pre_mix, w_in, b_gate, ssm_a_re, ssm_a_im, ssm_log_dt, ssm_b_re, ssm_b_im,
              ssm_c_re, ssm_c_im, ssm_d, w_glu, b_glu, lam_q1, lam_k1, lam_q2, lam_k2, g_sub,
              w_br_ssm, w_br_attn, w_out, g_post_mix, g_pre_ffn, w_ffn_gate, w_ffn_up,
              w_ffn_down, g_post_ffn):
    y_p = x_prompt
    y_s = x_sample
    kp_l, vp_l, hrp_l, hip_l = [], [], [], []
    ks_l, vs_l, hrs_l, his_l = [], [], [], []
    for l in range(DEPTH):
        prm = dict(g_pre_mix=g_pre_mix[l], w_in=w_in[l], b_gate=b_gate[l],
                   ssm_a_re=ssm_a_re[l], ssm_a_im=ssm_a_im[l], ssm_log_dt=ssm_log_dt[l],
                   ssm_b_re=ssm_b_re[l], ssm_b_im=ssm_b_im[l], ssm_c_re=ssm_c_re[l],
                   ssm_c_im=ssm_c_im[l], ssm_d=ssm_d[l], w_glu=w_glu[l], b_glu=b_glu[l],
                   lam_q1=lam_q1[l], lam_k1=lam_k1[l], lam_q2=lam_q2[l], lam_k2=lam_k2[l],
                   g_sub=g_sub[l], w_br_ssm=w_br_ssm[l], w_br_attn=w_br_attn[l], w_out=w_out[l],
                   g_post_mix=g_post_mix[l], g_pre_ffn=g_pre_ffn[l], w_ffn_gate=w_ffn_gate[l],
                   w_ffn_up=w_ffn_up[l], w_ffn_down=w_ffn_down[l], g_post_ffn=g_post_ffn[l])
        y_p, kp, vp, hrp, hip = layer_forward(y_p, l, prm, None, None, None, None)
        y_s, ksn, vsn, hrs, his = layer_forward(y_s, l, prm, state_ssm_re[l], state_ssm_im[l],
                                                cache_k[l], cache_v[l])
        kp_l.append(kp); vp_l.append(vp); hrp_l.append(hrp); hip_l.append(hip)
        ks_l.append(ksn); vs_l.append(vsn); hrs_l.append(hrs); his_l.append(his)
    k_prompt = jnp.stack(kp_l)
    v_prompt = jnp.stack(vp_l)
    ssm_re_prompt = jnp.stack(hrp_l)
    ssm_im_prompt = jnp.stack(hip_l)
    k_sample = jnp.stack(ks_l)
    v_sample = jnp.stack(vs_l)
    ssm_re_sample = jnp.stack(hrs_l)
    ssm_im_sample = jnp.stack(his_l)
    return (y_p, y_s, k_prompt, v_prompt, ssm_re_prompt, ssm_im_prompt,
            k_sample, v_sample, ssm_re_sample, ssm_im_sample)
```

```python
import functools
import math

import jax
import jax.numpy as jnp
from jax import lax
from jax.experimental import pallas as pl
from jax.experimental.pallas import tpu as pltpu

F32 = jnp.float32
BF16 = jnp.bfloat16

CHUNK = 64
SSM_GROUP = 16
STATE_DIM = 64
N_HEADS = 4
HEAD_DIM = 128
V_DIM = 2 * HEAD_DIM
N_BRANCH = 2
EPS = 1e-6
D_SSM = 1024
D_QK = N_HEADS * 2 * HEAD_DIM
D_ATTN = N_HEADS * V_DIM
assert D_SSM == D_QK == D_ATTN
COL_Q, COL_K, COL_V, COL_GATE = 1, 2, 3, 4

V7X_MXU_DIM = 256
V7X_SUBLANES = 8
V7X_VMEM_LIMIT = 56 * 1024 * 1024

SSM_KTILE = V7X_MXU_DIM
SSM_GROUPS_PER_TILE = SSM_KTILE // SSM_GROUP
SSM_STATE_PER_TILE = SSM_GROUPS_PER_TILE * STATE_DIM


def _params(sem, vmem=V7X_VMEM_LIMIT):
    return pltpu.CompilerParams(dimension_semantics=sem, vmem_limit_bytes=vmem)


def _rms(xf, g):
    ms = jnp.mean(xf * xf, axis=-1, keepdims=True)
    return xf * lax.rsqrt(ms + EPS) * g


def _sigmoid(x):
    return 1.0 / (1.0 + jnp.exp(-x))


def _gelu_tanh(x):
    c = math.sqrt(2.0 / math.pi)
    return x * (0.5 * (1.0 + jnp.tanh(c * (x + 0.044715 * (x * x * x)))))


def _lambda_init(layer):
    return 0.8 - 0.6 * math.exp(-0.3 * layer)


def _alibi_slope(h):
    return 2.0 ** (-8.0 * (h + 1) / N_HEADS)


def _norm_matmul_kernel(x_ref, g_ref, w_ref, o_ref, h_scr):
    @pl.when(pl.program_id(1) == 0)
    def _():
        h_scr[...] = _rms(x_ref[...], g_ref[...]).astype(BF16)

    o_ref[...] = jnp.dot(h_scr[...], w_ref[...], preferred_element_type=F32)


def norm_matmul(x, g, w_all, layer, tm, tn):
    m, d = x.shape
    n = w_all.shape[2]
    return pl.pallas_call(
        _norm_matmul_kernel,
        grid=(m // tm, n // tn),
        in_specs=[
            pl.BlockSpec((tm, d), lambda i, j: (i, 0)),
            pl.BlockSpec((1, d), lambda i, j: (0, 0)),
            pl.BlockSpec((None, d, tn), lambda i, j: (layer, 0, j)),
        ],
        out_specs=pl.BlockSpec((tm, tn), lambda i, j: (i, j)),
        out_shape=jax.ShapeDtypeStruct((m, n), F32),
        scratch_shapes=[pltpu.VMEM((tm, d), BF16)],
        compiler_params=_params(("parallel", "arbitrary")),
        name="norm_in_proj",
    )(x, g, w_all)


def _ssm_prep_kernel(are_ref, aim_ref, ldt_ref, bre_ref, bim_ref, cre_ref, cim_ref,
                     abre_ref, abim_ref, bc_ref, cc_ref):
    a_re = are_ref[...]
    a_im = aim_ref[...]
    dt = jnp.exp(ldt_ref[...])
    mag = jnp.exp(dt * a_re)
    ab_re = mag * jnp.cos(dt * a_im)
    ab_im = mag * jnp.sin(dt * a_im)
    n_re = ab_re - 1.0
    n_im = ab_im
    den = a_re * a_re + a_im * a_im
    c_re = (n_re * a_re + n_im * a_im) / den
    c_im = (n_im * a_re - n_re * a_im) / den
    abre_ref[...] = ab_re
    abim_ref[...] = ab_im
    b_re = bre_ref[...]
    b_im = bim_ref[...]
    ns = b_re.shape[1]
    bc_ref[:, :ns] = (c_re * b_re - c_im * b_im).astype(BF16)
    bc_ref[:, ns:] = (c_re * b_im + c_im * b_re).astype(BF16)
    cc_ref[:ns, :] = cre_ref[...].astype(BF16)
    cc_ref[ns:, :] = (-cim_ref[...]).astype(BF16)


def ssm_prep(a_re, a_im, log_dt, b_re, b_im, c_re, c_im):
    g, p = a_re.shape
    kt = g // SSM_GROUPS_PER_TILE
    ns = SSM_STATE_PER_TILE
    eye = jnp.eye(SSM_GROUPS_PER_TILE, dtype=F32)

    def place_b(b):
        bt = b.reshape(kt, SSM_GROUPS_PER_TILE, p, SSM_GROUP).transpose(0, 1, 3, 2)
        return jnp.einsum('kgcp,gh->kgchp', bt, eye).reshape(kt, SSM_KTILE, ns)

    def place_c(c):
        ct = c.reshape(kt, SSM_GROUPS_PER_TILE, SSM_GROUP, p).transpose(0, 1, 3, 2)
        return jnp.einsum('kgpc,gh->kgphc', ct, eye).reshape(kt, ns, SSM_KTILE)

    row = lambda a: a.reshape(1, g * p)
    ldt = row(jnp.broadcast_to(log_dt[:, None], (g, p)))
    vec = pl.BlockSpec((1, ns), lambda k: (0, k))
    bspec = pl.BlockSpec((None, SSM_KTILE, ns), lambda k: (k, 0, 0))
    cspec = pl.BlockSpec((None, ns, SSM_KTILE), lambda k: (k, 0, 0))
    return pl.pallas_call(
        _ssm_prep_kernel,
        grid=(kt,),
        in_specs=[vec, vec, vec, bspec, bspec, cspec, cspec],
        out_specs=[vec, vec,
                   pl.BlockSpec((None, SSM_KTILE, 2 * ns), lambda k: (k, 0, 0)),
                   pl.BlockSpec((None, 2 * ns, SSM_KTILE), lambda k: (k, 0, 0))],
        out_shape=[jax.ShapeDtypeStruct((1, g * p), F32),
                   jax.ShapeDtypeStruct((1, g * p), F32),
                   jax.ShapeDtypeStruct((kt, SSM_KTILE, 2 * ns), BF16),
                   jax.ShapeDtypeStruct((kt, 2 * ns, SSM_KTILE), BF16)],
        compiler_params=_params(("arbitrary",)),
        name="ssm_discretise",
    )(row(a_re), row(a_im), ldt, place_b(b_re), place_b(b_im), place_c(c_re), place_c(c_im))


SSM_SCAN_LANES = 512


def _ssm_kernel(u_ref, h0r_ref, h0i_ref, ar_ref, ai_ref, bc_ref, cc_ref, d_ref, wg_ref, bg_ref,
                out_ref, hr_ref, hi_ref, hs_scr, *, nb, n_kt):
    ns = SSM_STATE_PER_TILE
    rows = u_ref.shape[0]

    @pl.when(pl.program_id(0) == 0)
    def _():
        hr_ref[...] = h0r_ref[...]
        hi_ref[...] = h0i_ref[...]

    u = u_ref[...]
    ub = u.astype(BF16)
    for kt in range(n_kt):
        hs_scr[:, kt * 2 * ns:(kt + 1) * 2 * ns] = jnp.dot(
            ub[:, kt * SSM_KTILE:(kt + 1) * SSM_KTILE], bc_ref[kt], preferred_element_type=F32)

    rpi = hr_ref.shape[0]
    two_step = rpi != nb
    n_iter = rows // rpi
    w = SSM_SCAN_LANES
    low = lax.broadcasted_iota(jnp.int32, (rpi, w), 0) < nb
    for kt in range(n_kt):
        for s in range(ns // w):
            re0 = kt * 2 * ns + s * w
            im0 = re0 + ns
            a0 = kt * ns + s * w
            ar = jnp.broadcast_to(ar_ref[:, a0:a0 + w], (rpi, w))
            ai = jnp.broadcast_to(ai_ref[:, a0:a0 + w], (rpi, w))

            def body(it, carry, re0=re0, im0=im0, ar=ar, ai=ai):
                hr, hi = carry
                r0 = pl.multiple_of(it * rpi, rpi)
                br = hs_scr[pl.ds(r0, rpi), re0:re0 + w]
                bi = hs_scr[pl.ds(r0, rpi), im0:im0 + w]
                nr = ar * hr - ai * hi + br
                ni = ar * hi + ai * hr + bi
                if two_step:
                    dr = jnp.where(low, nr, pltpu.roll(nr, nb, 0))
                    di = jnp.where(low, ni, pltpu.roll(ni, nb, 0))
                    mr = ar * dr - ai * di + br
                    mi = ar * di + ai * dr + bi
                    hs_scr[pl.ds(r0, rpi), re0:re0 + w] = jnp.where(low, nr, mr)
                    hs_scr[pl.ds(r0, rpi), im0:im0 + w] = jnp.where(low, ni, mi)
                    nr = jnp.where(low, pltpu.roll(mr, nb, 0), mr)
                    ni = jnp.where(low, pltpu.roll(mi, nb, 0), mi)
                else:
                    hs_scr[pl.ds(r0, rpi), re0:re0 + w] = nr
                    hs_scr[pl.ds(r0, rpi), im0:im0 + w] = ni
                return nr, ni

            hr, hi = lax.fori_loop(0, n_iter, body,
                                   (hr_ref[:, a0:a0 + w], hi_ref[:, a0:a0 + w]))
            hr_ref[:, a0:a0 + w] = hr
            hi_ref[:, a0:a0 + w] = hi

    cols = []
    for kt in range(n_kt):
        hk = hs_scr[:, kt * 2 * ns:(kt + 1) * 2 * ns].astype(BF16)
        cols.append(jnp.dot(hk, cc_ref[kt], preferred_element_type=F32))
    y = jnp.concatenate(cols, axis=1) + d_ref[...] * u
    z = _gelu_tanh(y)
    gate = _sigmoid(jnp.dot(z.astype(BF16), wg_ref[...], preferred_element_type=F32) + bg_ref[...])
    out_ref[...] = (z * gate).astype(BF16)


def ssm_branch(proj, row_block0, n_chunks, rows, nb, h0_re, h0_im, ab_re, ab_im, bc, cc,
               d_row, w_glu_all, layer, b_glu_row):
    d_ssm = d_row.shape[1]
    n_kt = d_ssm // SSM_KTILE
    n_state = ab_re.shape[1]
    full = lambda a: pl.BlockSpec(a.shape, lambda c: (0,) * a.ndim)
    srows = max(nb, V7X_SUBLANES)
    assert srows == nb or srows == 2 * nb
    if srows != nb:
        h0_re = jnp.concatenate([h0_re, h0_re], axis=0)
        h0_im = jnp.concatenate([h0_im, h0_im], axis=0)
    a_out, h_re, h_im = pl.pallas_call(
        functools.partial(_ssm_kernel, nb=nb, n_kt=n_kt),
        grid=(n_chunks,),
        in_specs=[
            pl.BlockSpec((rows, d_ssm), lambda c: (row_block0 + c, 0)),
            full(h0_re), full(h0_im), full(ab_re), full(ab_im), full(bc), full(cc), full(d_row),
            pl.BlockSpec((None, d_ssm, d_ssm), lambda c: (layer, 0, 0)),
            full(b_glu_row),
        ],
        out_specs=[
            pl.BlockSpec((rows, d_ssm), lambda c: (c, 0)),
            pl.BlockSpec((srows, n_state), lambda c: (0, 0)),
            pl.BlockSpec((srows, n_state), lambda c: (0, 0)),
        ],
        out_shape=[
            jax.ShapeDtypeStruct((n_chunks * rows, d_ssm), BF16),
            jax.ShapeDtypeStruct((srows, n_state), F32),
            jax.ShapeDtypeStruct((srows, n_state), F32),
        ],
        scratch_shapes=[pltpu.VMEM((rows, 2 * n_state), F32)],
        compiler_params=_params(("arbitrary",)),
        name="ssm_scan",
    )(proj, h0_re, h0_im, ab_re, ab_im, bc, cc, d_row, w_glu_all, b_glu_row)
    return a_out, h_re[:nb], h_im[:nb]


def _diff_lambda(lq1_ref, lk1_ref, lq2_ref, lk2_ref, lam_init):
    s1 = jnp.sum(lq1_ref[...] * lk1_ref[...], axis=-1, keepdims=True)
    s2 = jnp.sum(lq2_ref[...] * lk2_ref[...], axis=-1, keepdims=True)
    return jnp.exp(s1) - jnp.exp(s2) + lam_init


def _masked_distance(qpos, kpos):
    dist = jnp.abs(qpos - kpos).astype(F32)
    shift = CHUNK.bit_length() - 1
    assert CHUNK == 1 << shift
    allowed = jnp.right_shift(kpos, shift) <= jnp.right_shift(qpos, shift)
    return jnp.where(allowed, dist, jnp.inf)


def _softmax_step(hm, s, v, m_scr, l_scr, acc_scr):
    m_old = m_scr[hm]
    m_new = jnp.maximum(m_old, jnp.max(s, axis=-1, keepdims=True))
    alpha = jnp.exp(m_old - m_new)
    p = jnp.exp(s - m_new)
    l_scr[hm] = alpha * l_scr[hm] + jnp.sum(p, axis=-1, keepdims=True)
    acc_scr[hm] = alpha * acc_scr[hm] + jnp.dot(p.astype(BF16), v, preferred_element_type=F32)
    m_scr[hm] = m_new


def _attn_tile(q_scr, k_tile, v_tile, distm, m_scr, l_scr, acc_scr):
    scale = HEAD_DIM ** -0.5
    for h in range(N_HEADS):
        bias = (-_alibi_slope(h)) * distm
        v = v_tile(h)
        for mp in range(2):
            hm = 2 * h + mp
            q = q_scr[:, hm * HEAD_DIM:(hm + 1) * HEAD_DIM]
            s = lax.dot_general(q, k_tile(hm), (((1,), (1,)), ((), ())),
                                preferred_element_type=F32) * scale + bias
            _softmax_step(hm, s, v, m_scr, l_scr, acc_scr)


def _attn_init(m_scr, l_scr, acc_scr):
    m_scr[...] = jnp.full(m_scr.shape, -jnp.inf, F32)
    l_scr[...] = jnp.zeros(l_scr.shape, F32)
    acc_scr[...] = jnp.zeros(acc_scr.shape, F32)


def _attn_finish(o_ref, lam, gsub_ref, lam_init, m_scr, l_scr, acc_scr):
    for h in range(N_HEADS):
        o = acc_scr[2 * h] / l_scr[2 * h] - lam * (acc_scr[2 * h + 1] / l_scr[2 * h + 1])
        o = _rms(o, gsub_ref[...]) * (1.0 - lam_init)
        o_ref[:, h * V_DIM:(h + 1) * V_DIM] = o.astype(BF16)


def _attn_prompt_kernel(q_ref, k_ref, v_ref, lq1_ref, lk1_ref, lq2_ref, lk2_ref, gsub_ref,
                        o_ref, q_scr, k_scr, v_scr, m_scr, l_scr, acc_scr, *, lam_init, tk):
    i = pl.program_id(1)
    tq = q_ref.shape[0]

    @pl.when(i == 0)
    def _():
        k_scr[...] = k_ref[...].astype(BF16)
        v_scr[...] = v_ref[...].astype(BF16)

    q_scr[...] = q_ref[...].astype(BF16)
    _attn_init(m_scr, l_scr, acc_scr)
    qpos = i * tq + lax.broadcasted_iota(jnp.int32, (tq, tk), 0)
    kcol = lax.broadcasted_iota(jnp.int32, (tq, tk), 1)

    def body(j, carry):
        k0 = pl.multiple_of(j * tk, tk)
        distm = _masked_distance(qpos, k0 + kcol)
        _attn_tile(
            q_scr,
            lambda hm: k_scr[pl.ds(k0, tk), hm * HEAD_DIM:(hm + 1) * HEAD_DIM],
            lambda h: v_scr[pl.ds(k0, tk), h * V_DIM:(h + 1) * V_DIM],
            distm, m_scr, l_scr, acc_scr)
        return carry

    n_tiles = ((i + 1) * tq + tk - 1) // tk
    lax.fori_loop(0, n_tiles, body, 0)
    lam = _diff_lambda(lq1_ref, lk1_ref, lq2_ref, lk2_ref, lam_init)
    _attn_finish(o_ref, lam, gsub_ref, lam_init, m_scr, l_scr, acc_scr)


def attn_prompt(proj, seq, batch, lam_rows, gsub_row, lam_init, tq, tk):
    rows, d_in = proj.shape
    nblk = d_in // D_QK
    view = proj.reshape(rows // batch, batch * d_in)
    lam_spec = pl.BlockSpec((1, HEAD_DIM), lambda b, i: (0, 0))
    out = pl.pallas_call(
        functools.partial(_attn_prompt_kernel, lam_init=lam_init, tk=tk),
        grid=(batch, seq // tq),
        in_specs=[
            pl.BlockSpec((tq, D_QK), lambda b, i: (i, b * nblk + COL_Q)),
            pl.BlockSpec((seq, D_QK), lambda b, i: (0, b * nblk + COL_K)),
            pl.BlockSpec((seq, D_ATTN), lambda b, i: (0, b * nblk + COL_V)),
            lam_spec, lam_spec, lam_spec, lam_spec,
            pl.BlockSpec((1, V_DIM), lambda b, i: (0, 0)),
        ],
        out_specs=pl.BlockSpec((tq, D_ATTN), lambda b, i: (i, b)),
        out_shape=jax.ShapeDtypeStruct((seq, batch * D_ATTN), BF16),
        scratch_shapes=[
            pltpu.VMEM((tq, D_QK), BF16),
            pltpu.VMEM((seq, D_QK), BF16),
            pltpu.VMEM((seq, D_ATTN), BF16),
            pltpu.VMEM((2 * N_HEADS, tq, 1), F32),
            pltpu.VMEM((2 * N_HEADS, tq, 1), F32),
            pltpu.VMEM((2 * N_HEADS, tq, V_DIM), F32),
        ],
        compiler_params=_params(("arbitrary", "arbitrary")),
        name="attn_prompt",
    )(view, view, view, *lam_rows, gsub_row)
    return out.reshape(seq * batch, D_ATTN)


def _attn_sample_kernel(q_ref, kn_ref, vn_ref, kp_ref, vp_ref, lq1_ref, lk1_ref, lq2_ref, lk2_ref,
                        gsub_ref, o_ref, q_scr, m_scr, l_scr, acc_scr, *, lam_init, past):
    j = pl.program_id(1)
    s_len = q_ref.shape[0]
    tk = kp_ref.shape[0]

    @pl.when(j == 0)
    def _():
        q_scr[...] = q_ref[...].astype(BF16)
        _attn_init(m_scr, l_scr, acc_scr)

    qpos = past + lax.broadcasted_iota(jnp.int32, (s_len, tk), 0)
    kpos = j * tk + lax.broadcasted_iota(jnp.int32, (s_len, tk), 1)
    _attn_tile(
        q_scr,
        lambda hm: kp_ref[:, hm * HEAD_DIM:(hm + 1) * HEAD_DIM].astype(BF16),
        lambda h: vp_ref[:, h * V_DIM:(h + 1) * V_DIM].astype(BF16),
        _masked_distance(qpos, kpos), m_scr, l_scr, acc_scr)

    @pl.when(j == pl.num_programs(1) - 1)
    def _():
        qn = past + lax.broadcasted_iota(jnp.int32, (s_len, s_len), 0)
        kn = past + lax.broadcasted_iota(jnp.int32, (s_len, s_len), 1)
        _attn_tile(
            q_scr,
            lambda hm: kn_ref[:, hm * HEAD_DIM:(hm + 1) * HEAD_DIM].astype(BF16),
            lambda h: vn_ref[:, h * V_DIM:(h + 1) * V_DIM].astype(BF16),
            _masked_distance(qn, kn), m_scr, l_scr, acc_scr)
        lam = _diff_lambda(lq1_ref, lk1_ref, lq2_ref, lk2_ref, lam_init)
        _attn_finish(o_ref, lam, gsub_ref, lam_init, m_scr, l_scr, acc_scr)


def attn_sample(proj, row0, s_len, batch, cache_k, cache_v, layer, lam_rows, gsub_row, lam_init, tk):
    rows, d_in = proj.shape
    nblk = d_in // D_QK
    past = cache_k.shape[2]
    view = proj.reshape(rows // batch, batch * d_in)
    rb = row0 // (batch * s_len)
    lam_spec = pl.BlockSpec((1, HEAD_DIM), lambda b, j: (0, 0))
    cache_spec = pl.BlockSpec((None, None, tk, D_QK), lambda b, j: (layer, b, j, 0))
    out = pl.pallas_call(
        functools.partial(_attn_sample_kernel, lam_init=lam_init, past=past),
        grid=(batch, past // tk),
        in_specs=[
            pl.BlockSpec((s_len, D_QK), lambda b, j: (rb, b * nblk + COL_Q)),
            pl.BlockSpec((s_len, D_QK), lambda b, j: (rb, b * nblk + COL_K)),
            pl.BlockSpec((s_len, D_ATTN), lambda b, j: (rb, b * nblk + COL_V)),
            cache_spec, cache_spec,
            lam_spec, lam_spec, lam_spec, lam_spec,
            pl.BlockSpec((1, V_DIM), lambda b, j: (0, 0)),
        ],
        out_specs=pl.BlockSpec((s_len, D_ATTN), lambda b, j: (0, b)),
        out_shape=jax.ShapeDtypeStruct((s_len, batch * D_ATTN), BF16),
        scratch_shapes=[
            pltpu.VMEM((s_len, D_QK), BF16),
            pltpu.VMEM((2 * N_HEADS, s_len, 1), F32),
            pltpu.VMEM((2 * N_HEADS, s_len, 1), F32),
            pltpu.VMEM((2 * N_HEADS, s_len, V_DIM), F32),
        ],
        compiler_params=_params(("arbitrary", "arbitrary")),
        name="attn_sample",
    )(view, view, view, cache_k, cache_v, *lam_rows, gsub_row)
    return out.reshape(s_len * batch, D_ATTN)


def _merge_kernel(a_ref, o_ref, wa_ref, wo_ref, ga_ref, go_ref, ba_ref, bo_ref, out_ref):
    ya = jnp.dot(a_ref[...], wa_ref[...], preferred_element_type=F32)
    yo = jnp.dot(o_ref[...], wo_ref[...], preferred_element_type=F32)
    merged = _sigmoid(ga_ref[...] + ba_ref[...]) * ya + _sigmoid(go_ref[...] + bo_ref[...]) * yo
    out_ref[...] = merged.astype(BF16)


def gated_merge(a_out, o_out, w_ssm_all, w_attn_all, proj, b_ssm_row, b_attn_row, layer, tm, tn):
    m, d_br = a_out.shape
    d_model = w_ssm_all.shape[2]
    gate0 = COL_GATE * D_QK // tn
    nj = d_model // tn
    return pl.pallas_call(
        _merge_kernel,
        grid=(m // tm, nj),
        in_specs=[
            pl.BlockSpec((tm, d_br), lambda i, j: (i, 0)),
            pl.BlockSpec((tm, d_br), lambda i, j: (i, 0)),
            pl.BlockSpec((None, d_br, tn), lambda i, j: (layer, 0, j)),
            pl.BlockSpec((None, d_br, tn), lambda i, j: (layer, 0, j)),
            pl.BlockSpec((tm, tn), lambda i, j: (i, gate0 + j)),
            pl.BlockSpec((tm, tn), lambda i, j: (i, gate0 + nj + j)),
            pl.BlockSpec((1, tn), lambda i, j: (0, j)),
            pl.BlockSpec((1, tn), lambda i, j: (0, j)),
        ],
        out_specs=pl.BlockSpec((tm, tn), lambda i, j: (i, j)),
        out_shape=jax.ShapeDtypeStruct((m, d_model), BF16),
        compiler_params=_params(("parallel", "arbitrary")),
        name="gated_merge",
    )(a_out, o_out, w_ssm_all, w_attn_all, proj, proj, b_ssm_row, b_attn_row)


def _out_proj_kernel(a_ref, w_ref, x_ref, g_ref, o_ref, *, tn):
    n = o_ref.shape[1]
    a = a_ref[...]
    ssq = jnp.zeros((a.shape[0], 1), F32)
    for c in range(n // tn):
        y = jnp.dot(a, w_ref[:, c * tn:(c + 1) * tn], preferred_element_type=F32)
        ssq = ssq + jnp.sum(y * y, axis=-1, keepdims=True)
        o_ref[:, c * tn:(c + 1) * tn] = y
    inv = lax.rsqrt(ssq / n + EPS)
    o_ref[...] = x_ref[...] + o_ref[...] * inv * g_ref[...]


def out_proj_residual(merged, w_all, x, g_row, layer, tm, tn):
    m, d = x.shape
    k = merged.shape[1]
    return pl.pallas_call(
        functools.partial(_out_proj_kernel, tn=tn),
        grid=(m // tm,),
        in_specs=[
            pl.BlockSpec((tm, k), lambda i: (i, 0)),
            pl.BlockSpec((None, k, d), lambda i: (layer, 0, 0)),
            pl.BlockSpec((tm, d), lambda i: (i, 0)),
            pl.BlockSpec((1, d), lambda i: (0, 0)),
        ],
        out_specs=pl.BlockSpec((tm, d), lambda i: (i, 0)),
        out_shape=jax.ShapeDtypeStruct((m, d), F32),
        compiler_params=_params(("parallel",)),
        name="out_proj_residual",
    )(merged, w_all, x, g_row)


def _ffn_kernel(x_ref, gpre_ref, wg_ref, wu_ref, wd_ref, gpost_ref, o_ref, h_scr):
    j = pl.program_id(1)

    @pl.when(j == 0)
    def _():
        h_scr[...] = _rms(x_ref[...], gpre_ref[...]).astype(BF16)

    h = h_scr[...]
    gate = jnp.dot(h, wg_ref[...], preferred_element_type=F32)
    up = jnp.dot(h, wu_ref[...], preferred_element_type=F32)
    act = (gate * _sigmoid(gate) * up).astype(BF16)
    contrib = jnp.dot(act, wd_ref[...], preferred_element_type=F32)

    @pl.when(j == 0)
    def _():
        o_ref[...] = contrib

    @pl.when(j > 0)
    def _():
        o_ref[...] += contrib

    @pl.when(j == pl.num_programs(1) - 1)
    def _():
        o_ref[...] = x_ref[...] + _rms(o_ref[...], gpost_ref[...])


def ffn_residual(x, gpre_row, wg_all, wu_all, wd_all, gpost_row, layer, tm, tf):
    m, d = x.shape
    f = wg_all.shape[2]
    return pl.pallas_call(
        _ffn_kernel,
        grid=(m // tm, f // tf),
        in_specs=[
            pl.BlockSpec((tm, d), lambda i, j: (i, 0)),
            pl.BlockSpec((1, d), lambda i, j: (0, 0)),
            pl.BlockSpec((None, d, tf), lambda i, j: (layer, 0, j)),
            pl.BlockSpec((None, d, tf), lambda i, j: (layer, 0, j)),
            pl.BlockSpec((None, tf, d), lambda i, j: (layer, j, 0)),
            pl.BlockSpec((1, d), lambda i, j: (0, 0)),
        ],
        out_specs=pl.BlockSpec((tm, d), lambda i, j: (i, 0)),
        out_shape=jax.ShapeDtypeStruct((m, d), F32),
        scratch_shapes=[pltpu.VMEM((tm, d), BF16)],
        compiler_params=_params(("parallel", "arbitrary")),
        name="swiglu_residual",
    )(x, gpre_row, wg_all, wu_all, wd_all, gpost_row)


def _tile(n, want):
    t = min(n, want)
    assert n % t == 0, (n, want)
    return t


def kernel(x_prompt, x_sample, cache_k, cache_v, state_ssm_re, state_ssm_im, g_pre_mix, w_in, b_gate, ssm_a_re, ssm_a_im, ssm_log_dt, ssm_b_re, ssm_b_im, ssm_c_re, ssm_c_im, ssm_d, w_glu, b_glu, lam_q1, lam_k1, lam_q2, lam_k2, g_sub, w_br_ssm, w_br_attn, w_out, g_post_mix, g_pre_ffn, w_ffn_gate, w_ffn_up, w_ffn_down, g_post_ffn):
    batch, seq, d_model = x_prompt.shape
    dec_batch, dec_seq, _ = x_sample.shape
    depth = w_in.shape[0]
    past = cache_k.shape[2]
    n_groups, state_dim = ssm_a_re.shape[1:]
    n_state = n_groups * state_dim
    assert ssm_d.shape[1] == D_SSM and w_in.shape[2] == COL_GATE * D_QK + N_BRANCH * d_model
    assert w_in.shape[2] % D_QK == 0
    rows_p = seq * batch
    rows_s = dec_seq * dec_batch

    x = jnp.concatenate([x_prompt.transpose(1, 0, 2).reshape(rows_p, d_model),
                         x_sample.transpose(1, 0, 2).reshape(rows_s, d_model)], axis=0)
    rows = rows_p + rows_s

    w_in_b = w_in.astype(BF16)
    w_glu_b = w_glu.astype(BF16)
    w_br_ssm_b = w_br_ssm.astype(BF16)
    w_br_attn_b = w_br_attn.astype(BF16)
    w_out_b = w_out.astype(BF16)
    w_gate_b = w_ffn_gate.astype(BF16)
    w_up_b = w_ffn_up.astype(BF16)
    w_down_b = w_ffn_down.astype(BF16)
    cache_k4 = cache_k.reshape(depth, dec_batch, past, D_QK)
    cache_v4 = cache_v.reshape(depth, dec_batch, past, D_ATTN)

    tm = _tile(rows_s, 1024)
    assert rows_p % tm == 0
    ssm_rows_p = _tile(rows_p, 256)
    ssm_rows_s = _tile(rows_s, 256)
    zeros_state = jnp.zeros((batch, n_state), F32)

    k_p, v_p, hr_p, hi_p, k_s, v_s, hr_s, hi_s = [], [], [], [], [], [], [], []
    for l in range(depth):
        lam_init = _lambda_init(l)
        proj = norm_matmul(x, g_pre_mix[l][None], w_in_b, l, tm, _tile(w_in.shape[2], 512))

        ab_re, ab_im, bc, cc = ssm_prep(ssm_a_re[l], ssm_a_im[l], ssm_log_dt[l], ssm_b_re[l],
                                        ssm_b_im[l], ssm_c_re[l], ssm_c_im[l])
        d_row = ssm_d[l][None]
        bg_row = b_glu[l][None]
        a_p, hrp, hip = ssm_branch(proj, 0, rows_p // ssm_rows_p, ssm_rows_p, batch, zeros_state,
                                   zeros_state, ab_re, ab_im, bc, cc, d_row, w_glu_b, l, bg_row)
        a_s, hrs, his = ssm_branch(proj, rows_p // ssm_rows_s, rows_s // ssm_rows_s, ssm_rows_s,
                                   dec_batch, state_ssm_re[l].reshape(dec_batch, n_state),
                                   state_ssm_im[l].reshape(dec_batch, n_state),
                                   ab_re, ab_im, bc, cc, d_row, w_glu_b, l, bg_row)

        lam_rows = (lam_q1[l][None], lam_k1[l][None], lam_q2[l][None], lam_k2[l][None])
        gsub_row = g_sub[l][None]
        o_p = attn_prompt(proj, seq, batch, lam_rows, gsub_row, lam_init,
                          _tile(seq, 256), _tile(seq, 256))
        o_s = attn_sample(proj, rows_p, dec_seq, dec_batch, cache_k4, cache_v4, l, lam_rows,
                          gsub_row, lam_init, _tile(past, 1024))

        a_out = jnp.concatenate([a_p, a_s], axis=0)
        o_out = jnp.concatenate([o_p, o_s], axis=0)
        merged = gated_merge(a_out, o_out, w_br_ssm_b, w_br_attn_b, proj, b_gate[l, 0][None],
                             b_gate[l, 1][None], l, tm, _tile(d_model, 512))
        x = out_proj_residual(merged, w_out_b, x, g_post_mix[l][None], l, _tile(rows_s, 512),
                              _tile(d_model, 512))
        x = ffn_residual(x, g_pre_ffn[l][None], w_gate_b, w_up_b, w_down_b, g_post_ffn[l][None],
                         l, _tile(rows_s, 512), _tile(w_ffn_gate.shape[2], 512))

        kcols = proj[:, COL_K * D_QK:(COL_K + 1) * D_QK]
        vcols = proj[:, COL_V * D_QK:(COL_V + 1) * D_QK]
        k_p.append(kcols[:rows_p].reshape(seq, batch, N_HEADS, 2, HEAD_DIM).transpose(1, 0, 2, 3, 4))
        v_p.append(vcols[:rows_p].reshape(seq, batch, N_HEADS, V_DIM).transpose(1, 0, 2, 3))
        k_s.append(kcols[rows_p:].reshape(dec_seq, dec_batch, N_HEADS, 2, HEAD_DIM).transpose(1, 0, 2, 3, 4))
        v_s.append(vcols[rows_p:].reshape(dec_seq, dec_batch, N_HEADS, V_DIM).transpose(1, 0, 2, 3))
        hr_p.append(hrp.reshape(batch, n_groups, state_dim))
        hi_p.append(hip.reshape(batch, n_groups, state_dim))
        hr_s.append(hrs.reshape(dec_batch, n_groups, state_dim))
        hi_s.append(his.reshape(dec_batch, n_groups, state_dim))

    y_p = x[:rows_p].reshape(seq, batch, d_model).transpose(1, 0, 2)
    y_s = x[rows_p:].reshape(dec_seq, dec_batch, d_model).transpose(1, 0, 2)
    return (y_p, y_s, jnp.stack(k_p), jnp.stack(v_p), jnp.stack(hr_p), jnp.stack(hi_p),
            jnp.stack(k_s), jnp.stack(v_s), jnp.stack(hr_s), jnp.stack(hi_s))
```

```python
import functools
import math

import jax
import jax.numpy as jnp
from jax import lax
from jax.experimental import pallas as pl
from jax.experimental.pallas import tpu as pltpu

F32 = jnp.float32
BF16 = jnp.bfloat16

CHUNK = 64
SSM_GROUP = 16
STATE_DIM = 64
N_HEADS = 4
HEAD_DIM = 128
V_DIM = 2 * HEAD_DIM
N_BRANCH = 2
EPS = 1e-6
D_SSM = 1024
D_QK = N_HEADS * 2 * HEAD_DIM
D_ATTN = N_HEADS * V_DIM
assert D_SSM == D_QK == D_ATTN
COL_Q, COL_K, COL_V, COL_GATE = 1, 2, 3, 4
CACHE_ROWS = D_QK // HEAD_DIM

V7X_MXU_DIM = 256
V7X_SUBLANES = 8
V7X_VMEM_LIMIT = 56 * 1024 * 1024

SSM_KTILE = V7X_MXU_DIM
SSM_GROUPS_PER_TILE = SSM_KTILE // SSM_GROUP
SSM_STATE_PER_TILE = SSM_GROUPS_PER_TILE * STATE_DIM


def _params(sem, vmem=V7X_VMEM_LIMIT):
    return pltpu.CompilerParams(dimension_semantics=sem, vmem_limit_bytes=vmem)


def _rms(xf, g):
    ms = jnp.mean(xf * xf, axis=-1, keepdims=True)
    return xf * lax.rsqrt(ms + EPS) * g


def _sigmoid(x):
    return 1.0 / (1.0 + jnp.exp(-x))


def _gelu_tanh(x):
    c = math.sqrt(2.0 / math.pi)
    return x * (0.5 * (1.0 + jnp.tanh(c * (x + 0.044715 * (x * x * x)))))


def _lambda_init(layer):
    return 0.8 - 0.6 * math.exp(-0.3 * layer)


def _alibi_slope(h):
    return 2.0 ** (-8.0 * (h + 1) / N_HEADS)


def _norm_matmul_kernel(x_ref, g_ref, w_ref, o_ref, h_scr):
    @pl.when(pl.program_id(1) == 0)
    def _():
        h_scr[...] = _rms(x_ref[...], g_ref[...]).astype(BF16)

    o_ref[...] = jnp.dot(h_scr[...], w_ref[...], preferred_element_type=F32)


def norm_matmul(x, g, w_all, layer, tm, tn):
    m, d = x.shape
    n = w_all.shape[2]
    return pl.pallas_call(
        _norm_matmul_kernel,
        grid=(m // tm, n // tn),
        in_specs=[
            pl.BlockSpec((tm, d), lambda i, j: (i, 0)),
            pl.BlockSpec((1, d), lambda i, j: (0, 0)),
            pl.BlockSpec((None, d, tn), lambda i, j: (layer, 0, j)),
        ],
        out_specs=pl.BlockSpec((tm, tn), lambda i, j: (i, j)),
        out_shape=jax.ShapeDtypeStruct((m, n), F32),
        scratch_shapes=[pltpu.VMEM((tm, d), BF16)],
        compiler_params=_params(("parallel", "arbitrary")),
        name="norm_in_proj",
    )(x, g, w_all)


def _ssm_prep_kernel(are_ref, aim_ref, ldt_ref, bre_ref, bim_ref, cre_ref, cim_ref,
                     abre_ref, abim_ref, bc_ref, cc_ref):
    a_re = are_ref[...]
    a_im = aim_ref[...]
    dt = jnp.exp(ldt_ref[...])
    mag = jnp.exp(dt * a_re)
    ab_re = mag * jnp.cos(dt * a_im)
    ab_im = mag * jnp.sin(dt * a_im)
    n_re = ab_re - 1.0
    n_im = ab_im
    den = a_re * a_re + a_im * a_im
    c_re = (n_re * a_re + n_im * a_im) / den
    c_im = (n_im * a_re - n_re * a_im) / den
    abre_ref[...] = ab_re
    abim_ref[...] = ab_im
    b_re = bre_ref[...]
    b_im = bim_ref[...]
    ns = b_re.shape[1]
    bc_ref[:, :ns] = (c_re * b_re - c_im * b_im).astype(BF16)
    bc_ref[:, ns:] = (c_re * b_im + c_im * b_re).astype(BF16)
    cc_ref[:ns, :] = cre_ref[...].astype(BF16)
    cc_ref[ns:, :] = (-cim_ref[...]).astype(BF16)


def ssm_prep(a_re, a_im, log_dt, b_re, b_im, c_re, c_im):
    g, p = a_re.shape
    kt = g // SSM_GROUPS_PER_TILE
    ns = SSM_STATE_PER_TILE
    eye = jnp.eye(SSM_GROUPS_PER_TILE, dtype=F32)

    def place_b(b):
        bt = b.reshape(kt, SSM_GROUPS_PER_TILE, p, SSM_GROUP).transpose(0, 1, 3, 2)
        return jnp.einsum('kgcp,gh->kgchp', bt, eye).reshape(kt, SSM_KTILE, ns)

    def place_c(c):
        ct = c.reshape(kt, SSM_GROUPS_PER_TILE, SSM_GROUP, p).transpose(0, 1, 3, 2)
        return jnp.einsum('kgpc,gh->kgphc', ct, eye).reshape(kt, ns, SSM_KTILE)

    row = lambda a: a.reshape(1, g * p)
    ldt = row(jnp.broadcast_to(log_dt[:, None], (g, p)))
    vec = pl.BlockSpec((1, ns), lambda k: (0, k))
    bspec = pl.BlockSpec((None, SSM_KTILE, ns), lambda k: (k, 0, 0))
    cspec = pl.BlockSpec((None, ns, SSM_KTILE), lambda k: (k, 0, 0))
    return pl.pallas_call(
        _ssm_prep_kernel,
        grid=(kt,),
        in_specs=[vec, vec, vec, bspec, bspec, cspec, cspec],
        out_specs=[vec, vec,
                   pl.BlockSpec((None, SSM_KTILE, 2 * ns), lambda k: (k, 0, 0)),
                   pl.BlockSpec((None, 2 * ns, SSM_KTILE), lambda k: (k, 0, 0))],
        out_shape=[jax.ShapeDtypeStruct((1, g * p), F32),
                   jax.ShapeDtypeStruct((1, g * p), F32),
                   jax.ShapeDtypeStruct((kt, SSM_KTILE, 2 * ns), BF16),
                   jax.ShapeDtypeStruct((kt, 2 * ns, SSM_KTILE), BF16)],
        compiler_params=_params(("arbitrary",)),
        name="ssm_discretise",
    )(row(a_re), row(a_im), ldt, place_b(b_re), place_b(b_im), place_c(c_re), place_c(c_im))


SSM_SCAN_LANES = 512


def _ssm_kernel(u_ref, h0r_ref, h0i_ref, ar_ref, ai_ref, bc_ref, cc_ref, d_ref, wg_ref, bg_ref,
                *refs, nb, n_kt):
    out_ref, hr_ref, hi_ref, hs_scr = refs[-4:]
    ns = SSM_STATE_PER_TILE
    rows = u_ref.shape[0]

    @pl.when(pl.program_id(0) == 0)
    def _():
        hr_ref[...] = h0r_ref[...]
        hi_ref[...] = h0i_ref[...]

    u = u_ref[...]
    ub = u.astype(BF16)
    for kt in range(n_kt):
        hs_scr[:, kt * 2 * ns:(kt + 1) * 2 * ns] = jnp.dot(
            ub[:, kt * SSM_KTILE:(kt + 1) * SSM_KTILE], bc_ref[kt], preferred_element_type=F32)

    rpi = hr_ref.shape[0]
    two_step = rpi != nb
    n_iter = rows // rpi
    w = SSM_SCAN_LANES
    low = lax.broadcasted_iota(jnp.int32, (rpi, w), 0) < nb
    for kt in range(n_kt):
        for s in range(ns // w):
            re0 = kt * 2 * ns + s * w
            im0 = re0 + ns
            a0 = kt * ns + s * w
            ar = jnp.broadcast_to(ar_ref[:, a0:a0 + w], (rpi, w))
            ai = jnp.broadcast_to(ai_ref[:, a0:a0 + w], (rpi, w))

            def body(it, carry, re0=re0, im0=im0, ar=ar, ai=ai):
                hr, hi = carry
                r0 = pl.multiple_of(it * rpi, rpi)
                br = hs_scr[pl.ds(r0, rpi), re0:re0 + w]
                bi = hs_scr[pl.ds(r0, rpi), im0:im0 + w]
                nr = ar * hr - ai * hi + br
                ni = ar * hi + ai * hr + bi
                if two_step:
                    sr = pltpu.roll(nr, nb, 0)
                    si = pltpu.roll(ni, nb, 0)
                    mr = ar * sr - ai * si + br
                    mi = ar * si + ai * sr + bi
                    hs_scr[pl.ds(r0, rpi), re0:re0 + w] = jnp.where(low, nr, mr)
                    hs_scr[pl.ds(r0, rpi), im0:im0 + w] = jnp.where(low, ni, mi)
                    nr = pltpu.roll(mr, nb, 0)
                    ni = pltpu.roll(mi, nb, 0)
                else:
                    hs_scr[pl.ds(r0, rpi), re0:re0 + w] = nr
                    hs_scr[pl.ds(r0, rpi), im0:im0 + w] = ni
                return nr, ni

            hr, hi = lax.fori_loop(0, n_iter, body,
                                   (hr_ref[:, a0:a0 + w], hi_ref[:, a0:a0 + w]))
            hr_ref[:, a0:a0 + w] = hr
            hi_ref[:, a0:a0 + w] = hi

    cols = []
    for kt in range(n_kt):
        hk = hs_scr[:, kt * 2 * ns:(kt + 1) * 2 * ns].astype(BF16)
        cols.append(jnp.dot(hk, cc_ref[kt], preferred_element_type=F32))
    y = jnp.concatenate(cols, axis=1) + d_ref[...] * u
    z = _gelu_tanh(y)
    gate = _sigmoid(jnp.dot(z.astype(BF16), wg_ref[...], preferred_element_type=F32) + bg_ref[...])
    out_ref[...] = (z * gate).astype(BF16)


def ssm_branch(proj, out_prev, row_block0, n_chunks, rows, nb, h0_re, h0_im, ab_re, ab_im, bc, cc,
               d_row, w_glu_all, layer, b_glu_row):
    d_ssm = d_row.shape[1]
    n_kt = d_ssm // SSM_KTILE
    n_state = ab_re.shape[1]
    full = lambda a: pl.BlockSpec(a.shape, lambda c: (0,) * a.ndim)
    srows = max(nb, V7X_SUBLANES)
    assert srows == nb or srows == 2 * nb
    if srows != nb:
        h0_re = jnp.concatenate([h0_re, h0_re], axis=0)
        h0_im = jnp.concatenate([h0_im, h0_im], axis=0)
    in_specs = [
        pl.BlockSpec((rows, d_ssm), lambda c: (row_block0 + c, 0)),
        full(h0_re), full(h0_im), full(ab_re), full(ab_im), full(bc), full(cc), full(d_row),
        pl.BlockSpec((None, d_ssm, d_ssm), lambda c: (layer, 0, 0)),
        full(b_glu_row),
    ]
    args = [proj, h0_re, h0_im, ab_re, ab_im, bc, cc, d_row, w_glu_all, b_glu_row]
    aliases = {}
    if out_prev is not None:
        in_specs.append(pl.BlockSpec(memory_space=pl.ANY))
        args.append(out_prev)
        aliases = {len(args) - 1: 0}
    a_out, h_re, h_im = pl.pallas_call(
        functools.partial(_ssm_kernel, nb=nb, n_kt=n_kt),
        grid=(n_chunks,),
        in_specs=in_specs,
        out_specs=[
            pl.BlockSpec((rows, d_ssm), lambda c: (row_block0 + c, 0)),
            pl.BlockSpec((srows, n_state), lambda c: (0, 0)),
            pl.BlockSpec((srows, n_state), lambda c: (0, 0)),
        ],
        out_shape=[
            jax.ShapeDtypeStruct((proj.shape[0], d_ssm), BF16),
            jax.ShapeDtypeStruct((srows, n_state), F32),
            jax.ShapeDtypeStruct((srows, n_state), F32),
        ],
        input_output_aliases=aliases,
        scratch_shapes=[pltpu.VMEM((rows, 2 * n_state), F32)],
        compiler_params=_params(("arbitrary",)),
        name="ssm_scan",
    )(*args)
    return a_out, h_re[:nb], h_im[:nb]


def _diff_lambda(lq1_ref, lk1_ref, lq2_ref, lk2_ref, lam_init):
    s1 = jnp.sum(lq1_ref[...] * lk1_ref[...], axis=-1, keepdims=True)
    s2 = jnp.sum(lq2_ref[...] * lk2_ref[...], axis=-1, keepdims=True)
    return jnp.exp(s1) - jnp.exp(s2) + lam_init


def _masked_distance(qpos, kpos):
    dist = jnp.abs(qpos - kpos).astype(F32)
    shift = CHUNK.bit_length() - 1
    assert CHUNK == 1 << shift
    allowed = jnp.right_shift(kpos, shift) <= jnp.right_shift(qpos, shift)
    return jnp.where(allowed, dist, jnp.inf)


def _softmax_step(hm, s, v, m_scr, l_scr, acc_scr):
    m_old = m_scr[hm]
    m_new = jnp.maximum(m_old, jnp.max(s, axis=-1, keepdims=True))
    alpha = jnp.exp(m_old - m_new)
    p = jnp.exp(s - m_new)
    l_scr[hm] = alpha * l_scr[hm] + jnp.sum(p, axis=-1, keepdims=True)
    acc_scr[hm] = alpha * acc_scr[hm] + jnp.dot(p.astype(BF16), v, preferred_element_type=F32)
    m_scr[hm] = m_new


def _attn_tile(q_scr, k_tile, v_tile, distm, m_scr, l_scr, acc_scr):
    scale = HEAD_DIM ** -0.5
    for h in range(N_HEADS):
        bias = (-_alibi_slope(h)) * distm
        v = v_tile(h)
        for mp in range(2):
            hm = 2 * h + mp
            q = q_scr[:, hm * HEAD_DIM:(hm + 1) * HEAD_DIM]
            s = lax.dot_general(q, k_tile(hm), (((1,), (1,)), ((), ())),
                                preferred_element_type=F32) * scale + bias
            _softmax_step(hm, s, v, m_scr, l_scr, acc_scr)


def _attn_init(m_scr, l_scr, acc_scr):
    m_scr[...] = jnp.full(m_scr.shape, -jnp.inf, F32)
    l_scr[...] = jnp.zeros(l_scr.shape, F32)
    acc_scr[...] = jnp.zeros(acc_scr.shape, F32)


def _attn_finish(o_ref, lam, gsub_ref, lam_init, m_scr, l_scr, acc_scr):
    for h in range(N_HEADS):
        o = acc_scr[2 * h] / l_scr[2 * h] - lam * (acc_scr[2 * h + 1] / l_scr[2 * h + 1])
        o = _rms(o, gsub_ref[...]) * (1.0 - lam_init)
        o_ref[:, h * V_DIM:(h + 1) * V_DIM] = o.astype(BF16)


def _fold_lanes(x, op):
    lanes = 128
    out = x[:, :lanes]
    for c in range(1, x.shape[1] // lanes):
        out = op(out, x[:, c * lanes:(c + 1) * lanes])
    return out


def _attn_prompt_kernel(q_ref, k_ref, v_ref, lq1_ref, lk1_ref, lq2_ref, lk2_ref, gsub_ref,
                        o_ref, q_scr, k_scr, v_scr, s_scr, m_scr, l_scr, acc_scr, *, lam_init):
    i = pl.program_id(1)
    tq = q_ref.shape[0]
    scale = HEAD_DIM ** -0.5

    @pl.when(i == 0)
    def _():
        k_scr[...] = k_ref[...].astype(BF16)
        v_scr[...] = v_ref[...].astype(BF16)

    q_scr[...] = q_ref[...].astype(BF16)
    row = lax.broadcasted_iota(jnp.int32, (tq, tq), 0)
    col = lax.broadcasted_iota(jnp.int32, (tq, tq), 1)
    rel = (row - col).astype(F32)
    dist_diag = _masked_distance(i * tq + row, i * tq + col)
    lam = _diff_lambda(lq1_ref, lk1_ref, lq2_ref, lk2_ref, lam_init)

    def scores(h, mp, k0, bias):
        hm = 2 * h + mp
        q = q_scr[:, hm * HEAD_DIM:(hm + 1) * HEAD_DIM]
        k = k_scr[pl.ds(k0, tq), hm * HEAD_DIM:(hm + 1) * HEAD_DIM]
        return lax.dot_general(q, k, (((1,), (1,)), ((), ())),
                               preferred_element_type=F32) * scale + bias

    for h in range(N_HEADS):
        neg_slope = -_alibi_slope(h)
        d0 = pl.multiple_of(i * tq, tq)
        bias_diag = neg_slope * dist_diag
        for mp in range(2):
            s = scores(h, mp, d0, bias_diag)
            s_scr[mp, :, pl.ds(d0, tq)] = s
            m_scr[mp] = _fold_lanes(s, jnp.maximum)
        base = neg_slope * rel

        def pass1(j, carry, h=h, base=base, neg_slope=neg_slope):
            k0 = pl.multiple_of(j * tq, tq)
            bias = base + neg_slope * ((i - j) * tq).astype(F32)
            for mp in range(2):
                s = scores(h, mp, k0, bias)
                s_scr[mp, :, pl.ds(k0, tq)] = s
                m_scr[mp] = jnp.maximum(m_scr[mp], _fold_lanes(s, jnp.maximum))
            return carry

        lax.fori_loop(0, i, pass1, 0)
        m_row = [jnp.max(m_scr[mp], axis=-1, keepdims=True) for mp in range(2)]
        l_scr[...] = jnp.zeros(l_scr.shape, F32)
        acc_scr[...] = jnp.zeros(acc_scr.shape, F32)

        def pass2(j, carry, h=h, m_row=m_row):
            k0 = pl.multiple_of(j * tq, tq)
            v = v_scr[pl.ds(k0, tq), h * V_DIM:(h + 1) * V_DIM]
            for mp in range(2):
                p = jnp.exp(s_scr[mp, :, pl.ds(k0, tq)] - m_row[mp])
                l_scr[mp] += _fold_lanes(p, jnp.add)
                acc_scr[mp] += jnp.dot(p.astype(BF16), v, preferred_element_type=F32)
            return carry

        lax.fori_loop(0, i + 1, pass2, 0)
        l_row = [jnp.sum(l_scr[mp], axis=-1, keepdims=True) for mp in range(2)]
        o = acc_scr[0] / l_row[0] - lam * (acc_scr[1] / l_row[1])
        o = _rms(o, gsub_ref[...]) * (1.0 - lam_init)
        o_ref[:, h * V_DIM:(h + 1) * V_DIM] = o.astype(BF16)


def attn_prompt(proj, seq, batch, lam_rows, gsub_row, lam_init, tq):
    rows, d_in = proj.shape
    nblk = d_in // D_QK
    view = proj.reshape(rows // batch, batch * d_in)
    lam_spec = pl.BlockSpec((1, HEAD_DIM), lambda b, i: (0, 0))
    return pl.pallas_call(
        functools.partial(_attn_prompt_kernel, lam_init=lam_init),
        grid=(batch, seq // tq),
        in_specs=[
            pl.BlockSpec((tq, D_QK), lambda b, i: (i, b * nblk + COL_Q)),
            pl.BlockSpec((seq, D_QK), lambda b, i: (0, b * nblk + COL_K),
                         pipeline_mode=pl.Buffered(1)),
            pl.BlockSpec((seq, D_ATTN), lambda b, i: (0, b * nblk + COL_V),
                         pipeline_mode=pl.Buffered(1)),
            lam_spec, lam_spec, lam_spec, lam_spec,
            pl.BlockSpec((1, V_DIM), lambda b, i: (0, 0)),
        ],
        out_specs=pl.BlockSpec((tq, D_ATTN), lambda b, i: (i, b)),
        out_shape=jax.ShapeDtypeStruct((rows // batch, batch * D_ATTN), BF16),
        scratch_shapes=[
            pltpu.VMEM((tq, D_QK), BF16),
            pltpu.VMEM((seq, D_QK), BF16),
            pltpu.VMEM((seq, D_ATTN), BF16),
            pltpu.VMEM((2, tq, seq), F32),
            pltpu.VMEM((2, tq, 128), F32),
            pltpu.VMEM((2, tq, 128), F32),
            pltpu.VMEM((2, tq, V_DIM), F32),
        ],
        compiler_params=_params(("arbitrary", "arbitrary")),
        name="attn_prompt",
    )(view, view, view, *lam_rows, gsub_row)


def _attn_sample_kernel(q_ref, kn_ref, vn_ref, kp_ref, vp_ref, lq1_ref, lk1_ref, lq2_ref, lk2_ref,
                        gsub_ref, o_prev_ref, o_ref, q_scr, m_scr, l_scr, acc_scr, *, lam_init, past):
    del o_prev_ref
    j = pl.program_id(1)
    s_len = q_ref.shape[0]
    tk = kp_ref.shape[0] // CACHE_ROWS

    @pl.when(j == 0)
    def _():
        q_scr[...] = q_ref[...].astype(BF16)
        _attn_init(m_scr, l_scr, acc_scr)

    def v_past(h):
        halves = [vp_ref[pl.ds(c * N_HEADS + h, tk, stride=CACHE_ROWS), :] for c in range(2)]
        return jnp.concatenate(halves, axis=1).astype(BF16)

    qpos = past + lax.broadcasted_iota(jnp.int32, (s_len, tk), 0)
    kpos = j * tk + lax.broadcasted_iota(jnp.int32, (s_len, tk), 1)
    _attn_tile(
        q_scr,
        lambda hm: kp_ref[pl.ds(hm, tk, stride=CACHE_ROWS), :].astype(BF16),
        v_past,
        _masked_distance(qpos, kpos), m_scr, l_scr, acc_scr)

    @pl.when(j == pl.num_programs(1) - 1)
    def _():
        qn = past + lax.broadcasted_iota(jnp.int32, (s_len, s_len), 0)
        kn = past + lax.broadcasted_iota(jnp.int32, (s_len, s_len), 1)
        _attn_tile(
            q_scr,
            lambda hm: kn_ref[:, hm * HEAD_DIM:(hm + 1) * HEAD_DIM].astype(BF16),
            lambda h: vn_ref[:, h * V_DIM:(h + 1) * V_DIM].astype(BF16),
            _masked_distance(qn, kn), m_scr, l_scr, acc_scr)
        lam = _diff_lambda(lq1_ref, lk1_ref, lq2_ref, lk2_ref, lam_init)
        _attn_finish(o_ref, lam, gsub_ref, lam_init, m_scr, l_scr, acc_scr)


def attn_sample(proj, o_prev, row0, s_len, batch, cache_k, cache_v, layer, lam_rows, gsub_row,
                lam_init, tk):
    rows, d_in = proj.shape
    nblk = d_in // D_QK
    depth, _, past = cache_k.shape[:3]
    view = proj.reshape(rows // batch, batch * d_in)
    rb = row0 // (batch * s_len)
    lam_spec = pl.BlockSpec((1, HEAD_DIM), lambda b, j: (0, 0))
    k_rows = cache_k.reshape(depth, batch, past * CACHE_ROWS, HEAD_DIM)
    v_rows = (cache_v.reshape(depth, batch, past, N_HEADS, 2, HEAD_DIM)
              .transpose(0, 1, 2, 4, 3, 5).reshape(depth, batch, past * CACHE_ROWS, HEAD_DIM))
    cache_spec = pl.BlockSpec((None, None, tk * CACHE_ROWS, HEAD_DIM), lambda b, j: (layer, b, j, 0))
    out = pl.pallas_call(
        functools.partial(_attn_sample_kernel, lam_init=lam_init, past=past),
        grid=(batch, past // tk),
        in_specs=[
            pl.BlockSpec((s_len, D_QK), lambda b, j: (rb, b * nblk + COL_Q)),
            pl.BlockSpec((s_len, D_QK), lambda b, j: (rb, b * nblk + COL_K)),
            pl.BlockSpec((s_len, D_ATTN), lambda b, j: (rb, b * nblk + COL_V)),
            cache_spec, cache_spec,
            lam_spec, lam_spec, lam_spec, lam_spec,
            pl.BlockSpec((1, V_DIM), lambda b, j: (0, 0)),
            pl.BlockSpec(memory_space=pl.ANY),
        ],
        out_specs=pl.BlockSpec((s_len, D_ATTN), lambda b, j: (rb, b)),
        out_shape=jax.ShapeDtypeStruct((rows // batch, batch * D_ATTN), BF16),
        input_output_aliases={10: 0},
        scratch_shapes=[
            pltpu.VMEM((s_len, D_QK), BF16),
            pltpu.VMEM((2 * N_HEADS, s_len, 1), F32),
            pltpu.VMEM((2 * N_HEADS, s_len, 1), F32),
            pltpu.VMEM((2 * N_HEADS, s_len, V_DIM), F32),
        ],
        compiler_params=_params(("arbitrary", "arbitrary")),
        name="attn_sample",
    )(view, view, view, k_rows, v_rows, *lam_rows, gsub_row,
      o_prev.reshape(rows // batch, batch * D_ATTN))
    return out.reshape(rows, D_ATTN)


def _kv_export_kernel(k_ref, v_ref, *refs):
    ko_ref, vo_ref = refs[-2:]
    for h in range(N_HEADS):
        vo_ref[:, h, :] = v_ref[:, h * V_DIM:(h + 1) * V_DIM]
        for mp in range(2):
            hm = 2 * h + mp
            ko_ref[:, h, mp, :] = k_ref[:, hm * HEAD_DIM:(hm + 1) * HEAD_DIM]


def kv_export(proj, row0, seq, batch, depth, layer, k_prev, v_prev, tr):
    rows, d_in = proj.shape
    nblk = d_in // D_QK
    view = proj.reshape(rows // batch, batch * d_in)
    rb = row0 // (batch * tr)
    in_specs = [
        pl.BlockSpec((tr, D_QK), lambda b, i: (rb + i, b * nblk + COL_K)),
        pl.BlockSpec((tr, D_ATTN), lambda b, i: (rb + i, b * nblk + COL_V)),
    ]
    args = [view, view]
    aliases = {}
    if k_prev is not None:
        in_specs += [pl.BlockSpec(memory_space=pl.ANY), pl.BlockSpec(memory_space=pl.ANY)]
        args += [k_prev, v_prev]
        aliases = {2: 0, 3: 1}
    return pl.pallas_call(
        _kv_export_kernel,
        grid=(batch, seq // tr),
        in_specs=in_specs,
        out_specs=[
            pl.BlockSpec((None, None, tr, N_HEADS, 2, HEAD_DIM), lambda b, i: (layer, b, i, 0, 0, 0)),
            pl.BlockSpec((None, None, tr, N_HEADS, V_DIM), lambda b, i: (layer, b, i, 0, 0)),
        ],
        out_shape=[
            jax.ShapeDtypeStruct((depth, batch, seq, N_HEADS, 2, HEAD_DIM), F32),
            jax.ShapeDtypeStruct((depth, batch, seq, N_HEADS, V_DIM), F32),
        ],
        input_output_aliases=aliases,
        compiler_params=_params(("arbitrary", "arbitrary")),
        name="kv_export",
    )(*args)


def _merge_kernel(a_ref, o_ref, wa_ref, wo_ref, ga_ref, go_ref, ba_ref, bo_ref, out_ref):
    ya = jnp.dot(a_ref[...], wa_ref[...], preferred_element_type=F32)
    yo = jnp.dot(o_ref[...], wo_ref[...], preferred_element_type=F32)
    merged = _sigmoid(ga_ref[...] + ba_ref[...]) * ya + _sigmoid(go_ref[...] + bo_ref[...]) * yo
    out_ref[...] = merged.astype(BF16)


def gated_merge(a_out, o_out, w_ssm_all, w_attn_all, proj, b_ssm_row, b_attn_row, layer, tm, tn):
    m, d_br = a_out.shape
    d_model = w_ssm_all.shape[2]
    gate0 = COL_GATE * D_QK // tn
    nj = d_model // tn
    return pl.pallas_call(
        _merge_kernel,
        grid=(m // tm, nj),
        in_specs=[
            pl.BlockSpec((tm, d_br), lambda i, j: (i, 0)),
            pl.BlockSpec((tm, d_br), lambda i, j: (i, 0)),
            pl.BlockSpec((None, d_br, tn), lambda i, j: (layer, 0, j)),
            pl.BlockSpec((None, d_br, tn), lambda i, j: (layer, 0, j)),
            pl.BlockSpec((tm, tn), lambda i, j: (i, gate0 + j)),
            pl.BlockSpec((tm, tn), lambda i, j: (i, gate0 + nj + j)),
            pl.BlockSpec((1, tn), lambda i, j: (0, j)),
            pl.BlockSpec((1, tn), lambda i, j: (0, j)),
        ],
        out_specs=pl.BlockSpec((tm, tn), lambda i, j: (i, j)),
        out_shape=jax.ShapeDtypeStruct((m, d_model), BF16),
        compiler_params=_params(("parallel", "arbitrary")),
        name="gated_merge",
    )(a_out, o_out, w_ssm_all, w_attn_all, proj, proj, b_ssm_row, b_attn_row)


def _out_proj_kernel(a_ref, w_ref, x_ref, g_ref, o_ref, *, tn):
    n = o_ref.shape[1]
    a = a_ref[...]
    ssq = jnp.zeros((a.shape[0], 1), F32)
    for c in range(n // tn):
        y = jnp.dot(a, w_ref[:, c * tn:(c + 1) * tn], preferred_element_type=F32)
        ssq = ssq + jnp.sum(y * y, axis=-1, keepdims=True)
        o_ref[:, c * tn:(c + 1) * tn] = y
    inv = lax.rsqrt(ssq / n + EPS)
    o_ref[...] = x_ref[...] + o_ref[...] * inv * g_ref[...]


def out_proj_residual(merged, w_all, x, g_row, layer, tm, tn):
    m, d = x.shape
    k = merged.shape[1]
    return pl.pallas_call(
        functools.partial(_out_proj_kernel, tn=tn),
        grid=(m // tm,),
        in_specs=[
            pl.BlockSpec((tm, k), lambda i: (i, 0)),
            pl.BlockSpec((None, k, d), lambda i: (layer, 0, 0)),
            pl.BlockSpec((tm, d), lambda i: (i, 0)),
            pl.BlockSpec((1, d), lambda i: (0, 0)),
        ],
        out_specs=pl.BlockSpec((tm, d), lambda i: (i, 0)),
        out_shape=jax.ShapeDtypeStruct((m, d), F32),
        compiler_params=_params(("parallel",)),
        name="out_proj_residual",
    )(merged, w_all, x, g_row)


def _ffn_kernel(x_ref, gpre_ref, wg_ref, wu_ref, wd_ref, gpost_ref, o_ref, h_scr):
    j = pl.program_id(1)

    @pl.when(j == 0)
    def _():
        h_scr[...] = _rms(x_ref[...], gpre_ref[...]).astype(BF16)

    h = h_scr[...]
    gate = jnp.dot(h, wg_ref[...], preferred_element_type=F32)
    up = jnp.dot(h, wu_ref[...], preferred_element_type=F32)
    act = (gate * _sigmoid(gate) * up).astype(BF16)
    contrib = jnp.dot(act, wd_ref[...], preferred_element_type=F32)

    @pl.when(j == 0)
    def _():
        o_ref[...] = contrib

    @pl.when(j > 0)
    def _():
        o_ref[...] += contrib

    @pl.when(j == pl.num_programs(1) - 1)
    def _():
        o_ref[...] = x_ref[...] + _rms(o_ref[...], gpost_ref[...])


def ffn_residual(x, gpre_row, wg_all, wu_all, wd_all, gpost_row, layer, tm, tf):
    m, d = x.shape
    f = wg_all.shape[2]
    return pl.pallas_call(
        _ffn_kernel,
        grid=(m // tm, f // tf),
        in_specs=[
            pl.BlockSpec((tm, d), lambda i, j: (i, 0)),
            pl.BlockSpec((1, d), lambda i, j: (0, 0)),
            pl.BlockSpec((None, d, tf), lambda i, j: (layer, 0, j)),
            pl.BlockSpec((None, d, tf), lambda i, j: (layer, 0, j)),
            pl.BlockSpec((None, tf, d), lambda i, j: (layer, j, 0)),
            pl.BlockSpec((1, d), lambda i, j: (0, 0)),
        ],
        out_specs=pl.BlockSpec((tm, d), lambda i, j: (i, 0)),
        out_shape=jax.ShapeDtypeStruct((m, d), F32),
        scratch_shapes=[pltpu.VMEM((tm, d), BF16)],
        compiler_params=_params(("parallel", "arbitrary")),
        name="swiglu_residual",
    )(x, gpre_row, wg_all, wu_all, wd_all, gpost_row)


def _tile(n, want):
    t = min(n, want)
    assert n % t == 0, (n, want)
    return t


def kernel(x_prompt, x_sample, cache_k, cache_v, state_ssm_re, state_ssm_im, g_pre_mix, w_in, b_gate, ssm_a_re, ssm_a_im, ssm_log_dt, ssm_b_re, ssm_b_im, ssm_c_re, ssm_c_im, ssm_d, w_glu, b_glu, lam_q1, lam_k1, lam_q2, lam_k2, g_sub, w_br_ssm, w_br_attn, w_out, g_post_mix, g_pre_ffn, w_ffn_gate, w_ffn_up, w_ffn_down, g_post_ffn):
    batch, seq, d_model = x_prompt.shape
    dec_batch, dec_seq, _ = x_sample.shape
    depth = w_in.shape[0]
    past = cache_k.shape[2]
    n_groups, state_dim = ssm_a_re.shape[1:]
    n_state = n_groups * state_dim
    assert ssm_d.shape[1] == D_SSM and w_in.shape[2] == COL_GATE * D_QK + N_BRANCH * d_model
    assert w_in.shape[2] % D_QK == 0
    rows_p = seq * batch
    rows_s = dec_seq * dec_batch

    x = jnp.concatenate([x_prompt.transpose(1, 0, 2).reshape(rows_p, d_model),
                         x_sample.transpose(1, 0, 2).reshape(rows_s, d_model)], axis=0)
    rows = rows_p + rows_s

    w_in_b = w_in.astype(BF16)
    w_glu_b = w_glu.astype(BF16)
    w_br_ssm_b = w_br_ssm.astype(BF16)
    w_br_attn_b = w_br_attn.astype(BF16)
    w_out_b = w_out.astype(BF16)
    w_gate_b = w_ffn_gate.astype(BF16)
    w_up_b = w_ffn_up.astype(BF16)
    w_down_b = w_ffn_down.astype(BF16)

    tm = _tile(rows_s, 1024)
    assert rows_p % tm == 0
    ssm_rows_p = _tile(rows_p, 256)
    ssm_rows_s = _tile(rows_s, 256)
    zeros_state = jnp.zeros((batch, n_state), F32)

    hr_p, hi_p, hr_s, hi_s = [], [], [], []
    k_p = v_p = k_s = v_s = None
    for l in range(depth):
        lam_init = _lambda_init(l)
        proj = norm_matmul(x, g_pre_mix[l][None], w_in_b, l, tm, _tile(w_in.shape[2], 512))

        ab_re, ab_im, bc, cc = ssm_prep(ssm_a_re[l], ssm_a_im[l], ssm_log_dt[l], ssm_b_re[l],
                                        ssm_b_im[l], ssm_c_re[l], ssm_c_im[l])
        d_row = ssm_d[l][None]
        bg_row = b_glu[l][None]
        a_out, hrp, hip = ssm_branch(proj, None, 0, rows_p // ssm_rows_p, ssm_rows_p, batch,
                                     zeros_state, zeros_state, ab_re, ab_im, bc, cc, d_row,
                                     w_glu_b, l, bg_row)
        a_out, hrs, his = ssm_branch(proj, a_out, rows_p // ssm_rows_s, rows_s // ssm_rows_s,
                                     ssm_rows_s, dec_batch,
                                     state_ssm_re[l].reshape(dec_batch, n_state),
                                     state_ssm_im[l].reshape(dec_batch, n_state),
                                     ab_re, ab_im, bc, cc, d_row, w_glu_b, l, bg_row)

        lam_rows = (lam_q1[l][None], lam_k1[l][None], lam_q2[l][None], lam_k2[l][None])
        gsub_row = g_sub[l][None]
        o_out = attn_prompt(proj, seq, batch, lam_rows, gsub_row, lam_init, _tile(seq, 512))
        o_out = attn_sample(proj, o_out, rows_p, dec_seq, dec_batch, cache_k, cache_v, l, lam_rows,
                            gsub_row, lam_init, _tile(past, 1024))
        k_p, v_p = kv_export(proj, 0, seq, batch, depth, l, k_p, v_p, _tile(seq, 512))
        k_s, v_s = kv_export(proj, rows_p, dec_seq, dec_batch, depth, l, k_s, v_s, dec_seq)

        merged = gated_merge(a_out, o_out, w_br_ssm_b, w_br_attn_b, proj, b_gate[l, 0][None],
                             b_gate[l, 1][None], l, tm, _tile(d_model, 512))
        x = out_proj_residual(merged, w_out_b, x, g_post_mix[l][None], l, _tile(rows_s, 512),
                              _tile(d_model, 512))
        x = ffn_residual(x, g_pre_ffn[l][None], w_gate_b, w_up_b, w_down_b, g_post_ffn[l][None],
                         l, _tile(rows_s, 512), _tile(w_ffn_gate.shape[2], 512))

        hr_p.append(hrp.reshape(batch, n_groups, state_dim))
        hi_p.append(hip.reshape(batch, n_groups, state_dim))
        hr_s.append(hrs.reshape(dec_batch, n_groups, state_dim))
        hi_s.append(his.reshape(dec_batch, n_groups, state_dim))

    y_p = x[:rows_p].reshape(seq, batch, d_model).transpose(1, 0, 2)
    y_s = x[rows_p:].reshape(dec_seq, dec_batch, d_model).transpose(1, 0, 2)
    return (y_p, y_s, k_p, v_p, jnp.stack(hr_p), jnp.stack(hi_p),
            k_s, v_s, jnp.stack(hr_s), jnp.stack(hi_s))
```

```python
import functools
import math

import jax
import jax.numpy as jnp
from jax import lax
from jax.experimental import pallas as pl
from jax.experimental.pallas import tpu as pltpu

F32 = jnp.float32
BF16 = jnp.bfloat16

CHUNK = 64
SSM_GROUP = 16
STATE_DIM = 64
N_HEADS = 4
HEAD_DIM = 128
V_DIM = 2 * HEAD_DIM
N_BRANCH = 2
EPS = 1e-6
D_SSM = 1024
D_QK = N_HEADS * 2 * HEAD_DIM
D_ATTN = N_HEADS * V_DIM
assert D_SSM == D_QK == D_ATTN
COL_Q, COL_K, COL_V, COL_GATE = 1, 2, 3, 4
CACHE_ROWS = D_QK // HEAD_DIM

V7X_MXU_DIM = 256
V7X_SUBLANES = 8
LANES = 128
V7X_VMEM_LIMIT = 56 * 1024 * 1024

SSM_KTILE = V7X_MXU_DIM
SSM_GROUPS_PER_TILE = SSM_KTILE // SSM_GROUP
SSM_STATE_PER_TILE = SSM_GROUPS_PER_TILE * STATE_DIM


def _params(sem, vmem=V7X_VMEM_LIMIT):
    return pltpu.CompilerParams(dimension_semantics=sem, vmem_limit_bytes=vmem)


def _rms(xf, g):
    ms = jnp.mean(xf * xf, axis=-1, keepdims=True)
    return xf * lax.rsqrt(ms + EPS) * g


def _sigmoid(x):
    return 1.0 / (1.0 + jnp.exp(-x))


def _gelu_tanh(x):
    c = math.sqrt(2.0 / math.pi)
    return x * (0.5 * (1.0 + jnp.tanh(c * (x + 0.044715 * (x * x * x)))))


def _lambda_init(layer):
    return 0.8 - 0.6 * math.exp(-0.3 * layer)


def _alibi_slope(h):
    return 2.0 ** (-8.0 * (h + 1) / N_HEADS)


def _norm_matmul_kernel(x_ref, g_ref, w_ref, o_ref, h_scr):
    @pl.when(pl.program_id(1) == 0)
    def _():
        h_scr[...] = _rms(x_ref[...], g_ref[...]).astype(BF16)

    o_ref[...] = jnp.dot(h_scr[...], w_ref[...], preferred_element_type=F32)


def norm_matmul(x, g, w_all, layer, tm, tn):
    m, d = x.shape
    n = w_all.shape[2]
    return pl.pallas_call(
        _norm_matmul_kernel,
        grid=(m // tm, n // tn),
        in_specs=[
            pl.BlockSpec((tm, d), lambda i, j: (i, 0)),
            pl.BlockSpec((1, d), lambda i, j: (0, 0)),
            pl.BlockSpec((None, d, tn), lambda i, j: (layer, 0, j)),
        ],
        out_specs=pl.BlockSpec((tm, tn), lambda i, j: (i, j)),
        out_shape=jax.ShapeDtypeStruct((m, n), F32),
        scratch_shapes=[pltpu.VMEM((tm, d), BF16)],
        compiler_params=_params(("parallel", "arbitrary")),
        name="norm_in_proj",
    )(x, g, w_all)


def _ssm_prep_kernel(are_ref, aim_ref, ldt_ref, bre_ref, bim_ref, cre_ref, cim_ref,
                     abre_ref, abim_ref, bc_ref, cc_ref):
    a_re = are_ref[...]
    a_im = aim_ref[...]
    dt = jnp.exp(ldt_ref[...])
    mag = jnp.exp(dt * a_re)
    ab_re = mag * jnp.cos(dt * a_im)
    ab_im = mag * jnp.sin(dt * a_im)
    n_re = ab_re - 1.0
    n_im = ab_im
    den = a_re * a_re + a_im * a_im
    c_re = (n_re * a_re + n_im * a_im) / den
    c_im = (n_im * a_re - n_re * a_im) / den
    abre_ref[...] = ab_re
    abim_ref[...] = ab_im
    b_re = bre_ref[...]
    b_im = bim_ref[...]
    ns = b_re.shape[1]
    bc_ref[:, :ns] = (c_re * b_re - c_im * b_im).astype(BF16)
    bc_ref[:, ns:] = (c_re * b_im + c_im * b_re).astype(BF16)
    cc_ref[:ns, :] = cre_ref[...].astype(BF16)
    cc_ref[ns:, :] = (-cim_ref[...]).astype(BF16)


def ssm_prep(a_re, a_im, log_dt, b_re, b_im, c_re, c_im):
    g, p = a_re.shape
    kt = g // SSM_GROUPS_PER_TILE
    ns = SSM_STATE_PER_TILE
    eye = jnp.eye(SSM_GROUPS_PER_TILE, dtype=F32)

    def place_b(b):
        bt = b.reshape(kt, SSM_GROUPS_PER_TILE, p, SSM_GROUP).transpose(0, 1, 3, 2)
        return jnp.einsum('kgcp,gh->kgchp', bt, eye).reshape(kt, SSM_KTILE, ns)

    def place_c(c):
        ct = c.reshape(kt, SSM_GROUPS_PER_TILE, SSM_GROUP, p).transpose(0, 1, 3, 2)
        return jnp.einsum('kgpc,gh->kgphc', ct, eye).reshape(kt, ns, SSM_KTILE)

    row = lambda a: a.reshape(1, g * p)
    ldt = row(jnp.broadcast_to(log_dt[:, None], (g, p)))
    vec = pl.BlockSpec((1, ns), lambda k: (0, k))
    bspec = pl.BlockSpec((None, SSM_KTILE, ns), lambda k: (k, 0, 0))
    cspec = pl.BlockSpec((None, ns, SSM_KTILE), lambda k: (k, 0, 0))
    return pl.pallas_call(
        _ssm_prep_kernel,
        grid=(kt,),
        in_specs=[vec, vec, vec, bspec, bspec, cspec, cspec],
        out_specs=[vec, vec,
                   pl.BlockSpec((None, SSM_KTILE, 2 * ns), lambda k: (k, 0, 0)),
                   pl.BlockSpec((None, 2 * ns, SSM_KTILE), lambda k: (k, 0, 0))],
        out_shape=[jax.ShapeDtypeStruct((1, g * p), F32),
                   jax.ShapeDtypeStruct((1, g * p), F32),
                   jax.ShapeDtypeStruct((kt, SSM_KTILE, 2 * ns), BF16),
                   jax.ShapeDtypeStruct((kt, 2 * ns, SSM_KTILE), BF16)],
        compiler_params=_params(("arbitrary",)),
        name="ssm_discretise",
    )(row(a_re), row(a_im), ldt, place_b(b_re), place_b(b_im), place_c(c_re), place_c(c_im))


SSM_SCAN_LANES = 512


def _ssm_kernel(*refs, nb, n_kt):
    u_refs = refs[:nb]
    (h0r_ref, h0i_ref, ar_ref, ai_ref, bc_ref, cc_ref, d_ref, wg_ref, bg_ref,
     out_ref, hr_ref, hi_ref, io_scr, hs_scr) = refs[nb:]
    ns = SSM_STATE_PER_TILE
    tc, d_ssm = u_refs[0].shape
    rows = nb * tc
    n_lt = d_ssm // LANES

    @pl.when(pl.program_id(0) == 0)
    def _():
        hr_ref[...] = h0r_ref[...]
        hi_ref[...] = h0i_ref[...]

    for b in range(nb):
        for j in range(n_lt):
            io_scr[j, pl.ds(b, tc, stride=nb), :] = u_refs[b][:, j * LANES:(j + 1) * LANES]
    u = jnp.concatenate([io_scr[j] for j in range(n_lt)], axis=1)
    ub = u.astype(BF16)
    for kt in range(n_kt):
        hs_scr[:, kt * 2 * ns:(kt + 1) * 2 * ns] = jnp.dot(
            ub[:, kt * SSM_KTILE:(kt + 1) * SSM_KTILE], bc_ref[kt], preferred_element_type=F32)

    rpi = hr_ref.shape[0]
    two_step = rpi != nb
    n_iter = rows // rpi
    w = SSM_SCAN_LANES
    low = lax.broadcasted_iota(jnp.int32, (rpi, w), 0) < nb
    for kt in range(n_kt):
        for s in range(ns // w):
            re0 = kt * 2 * ns + s * w
            im0 = re0 + ns
            a0 = kt * ns + s * w
            ar = jnp.broadcast_to(ar_ref[:, a0:a0 + w], (rpi, w))
            ai = jnp.broadcast_to(ai_ref[:, a0:a0 + w], (rpi, w))

            def body(it, carry, re0=re0, im0=im0, ar=ar, ai=ai):
                hr, hi = carry
                r0 = pl.multiple_of(it * rpi, rpi)
                br = hs_scr[pl.ds(r0, rpi), re0:re0 + w]
                bi = hs_scr[pl.ds(r0, rpi), im0:im0 + w]
                nr = ar * hr - ai * hi + br
                ni = ar * hi + ai * hr + bi
                if two_step:
                    sr = pltpu.roll(nr, nb, 0)
                    si = pltpu.roll(ni, nb, 0)
                    mr = ar * sr - ai * si + br
                    mi = ar * si + ai * sr + bi
                    hs_scr[pl.ds(r0, rpi), re0:re0 + w] = jnp.where(low, nr, mr)
                    hs_scr[pl.ds(r0, rpi), im0:im0 + w] = jnp.where(low, ni, mi)
                    nr = pltpu.roll(mr, nb, 0)
                    ni = pltpu.roll(mi, nb, 0)
                else:
                    hs_scr[pl.ds(r0, rpi), re0:re0 + w] = nr
                    hs_scr[pl.ds(r0, rpi), im0:im0 + w] = ni
                return nr, ni

            hr, hi = lax.fori_loop(0, n_iter, body,
                                   (hr_ref[:, a0:a0 + w], hi_ref[:, a0:a0 + w]))
            hr_ref[:, a0:a0 + w] = hr
            hi_ref[:, a0:a0 + w] = hi

    cols = []
    for kt in range(n_kt):
        hk = hs_scr[:, kt * 2 * ns:(kt + 1) * 2 * ns].astype(BF16)
        cols.append(jnp.dot(hk, cc_ref[kt], preferred_element_type=F32))
    y = jnp.concatenate(cols, axis=1) + d_ref[...] * u
    z = _gelu_tanh(y)
    gate = _sigmoid(jnp.dot(z.astype(BF16), wg_ref[...], preferred_element_type=F32) + bg_ref[...])
    out = z * gate
    for j in range(n_lt):
        io_scr[j] = out[:, j * LANES:(j + 1) * LANES]
    for b in range(nb):
        out_ref[b] = jnp.concatenate(
            [io_scr[j, pl.ds(b, tc, stride=nb), :] for j in range(n_lt)], axis=1).astype(BF16)


def ssm_branch(proj, row0, seq, nb, tc, h0_re, h0_im, ab_re, ab_im, bc, cc, d_row, w_glu_all,
               layer, b_glu_row):
    d_ssm = d_row.shape[1]
    n_kt = d_ssm // SSM_KTILE
    n_state = ab_re.shape[1]
    full = lambda a: pl.BlockSpec(a.shape, lambda c: (0,) * a.ndim)
    srows = max(nb, V7X_SUBLANES)
    assert srows == nb or srows == 2 * nb
    if srows != nb:
        h0_re = jnp.concatenate([h0_re, h0_re], axis=0)
        h0_im = jnp.concatenate([h0_im, h0_im], axis=0)
    n_chunks = seq // tc
    rb0 = row0 // tc
    u_specs = [pl.BlockSpec((tc, d_ssm), lambda c, b=b: (rb0 + b * n_chunks + c, 0))
               for b in range(nb)]
    a_out, h_re, h_im = pl.pallas_call(
        functools.partial(_ssm_kernel, nb=nb, n_kt=n_kt),
        grid=(n_chunks,),
        in_specs=u_specs + [
            full(h0_re), full(h0_im), full(ab_re), full(ab_im), full(bc), full(cc), full(d_row),
            pl.BlockSpec((None, d_ssm, d_ssm), lambda c: (layer, 0, 0)),
            full(b_glu_row),
        ],
        out_specs=[
            pl.BlockSpec((nb, tc, d_ssm), lambda c: (0, c, 0)),
            pl.BlockSpec((srows, n_state), lambda c: (0, 0)),
            pl.BlockSpec((srows, n_state), lambda c: (0, 0)),
        ],
        out_shape=[
            jax.ShapeDtypeStruct((nb, seq, d_ssm), BF16),
            jax.ShapeDtypeStruct((srows, n_state), F32),
            jax.ShapeDtypeStruct((srows, n_state), F32),
        ],
        scratch_shapes=[pltpu.VMEM((d_ssm // LANES, nb * tc, LANES), F32),
                        pltpu.VMEM((nb * tc, 2 * n_state), F32)],
        compiler_params=_params(("arbitrary",)),
        name="ssm_scan",
    )(*([proj] * nb), h0_re, h0_im, ab_re, ab_im, bc, cc, d_row, w_glu_all, b_glu_row)
    return a_out.reshape(nb * seq, d_ssm), h_re[:nb], h_im[:nb]


def _diff_lambda(lq1_ref, lk1_ref, lq2_ref, lk2_ref, lam_init):
    s1 = jnp.sum(lq1_ref[...] * lk1_ref[...], axis=-1, keepdims=True)
    s2 = jnp.sum(lq2_ref[...] * lk2_ref[...], axis=-1, keepdims=True)
    return jnp.exp(s1) - jnp.exp(s2) + lam_init


def _masked_distance(qpos, kpos):
    dist = jnp.abs(qpos - kpos).astype(F32)
    shift = CHUNK.bit_length() - 1
    assert CHUNK == 1 << shift
    allowed = jnp.right_shift(kpos, shift) <= jnp.right_shift(qpos, shift)
    return jnp.where(allowed, dist, jnp.inf)


def _softmax_step(hm, s, v, m_scr, l_scr, acc_scr):
    m_old = m_scr[hm]
    m_new = jnp.maximum(m_old, jnp.max(s, axis=-1, keepdims=True))
    alpha = jnp.exp(m_old - m_new)
    p = jnp.exp(s - m_new)
    l_scr[hm] = alpha * l_scr[hm] + jnp.sum(p, axis=-1, keepdims=True)
    acc_scr[hm] = alpha * acc_scr[hm] + jnp.dot(p.astype(BF16), v, preferred_element_type=F32)
    m_scr[hm] = m_new


def _attn_tile(q_scr, k_tile, v_tile, distm, m_scr, l_scr, acc_scr):
    scale = HEAD_DIM ** -0.5
    for h in range(N_HEADS):
        bias = (-_alibi_slope(h)) * distm
        v = v_tile(h)
        for mp in range(2):
            hm = 2 * h + mp
            q = q_scr[:, hm * HEAD_DIM:(hm + 1) * HEAD_DIM]
            s = lax.dot_general(q, k_tile(hm), (((1,), (1,)), ((), ())),
                                preferred_element_type=F32) * scale + bias
            _softmax_step(hm, s, v, m_scr, l_scr, acc_scr)


def _attn_init(m_scr, l_scr, acc_scr):
    m_scr[...] = jnp.full(m_scr.shape, -jnp.inf, F32)
    l_scr[...] = jnp.zeros(l_scr.shape, F32)
    acc_scr[...] = jnp.zeros(acc_scr.shape, F32)


def _attn_finish(o_ref, lam, gsub_ref, lam_init, m_scr, l_scr, acc_scr):
    for h in range(N_HEADS):
        o = acc_scr[2 * h] / l_scr[2 * h] - lam * (acc_scr[2 * h + 1] / l_scr[2 * h + 1])
        o = _rms(o, gsub_ref[...]) * (1.0 - lam_init)
        o_ref[:, h * V_DIM:(h + 1) * V_DIM] = o.astype(BF16)


def _fold_lanes(x, op):
    out = x[:, :LANES]
    for c in range(1, x.shape[1] // LANES):
        out = op(out, x[:, c * LANES:(c + 1) * LANES])
    return out


def _attn_prompt_kernel(q_ref, k_ref, v_ref, lq1_ref, lk1_ref, lq2_ref, lk2_ref, gsub_ref,
                        o_ref, q_scr, k_scr, v_scr, s_scr, m_scr, l_scr, acc_scr, *, lam_init):
    i = pl.program_id(1)
    tq = q_ref.shape[0]
    scale = HEAD_DIM ** -0.5

    @pl.when(i == 0)
    def _():
        k_scr[...] = k_ref[...].astype(BF16)
        v_scr[...] = v_ref[...].astype(BF16)

    q_scr[...] = q_ref[...].astype(BF16)
    row = lax.broadcasted_iota(jnp.int32, (tq, tq), 0)
    col = lax.broadcasted_iota(jnp.int32, (tq, tq), 1)
    rel = (row - col).astype(F32)
    dist_diag = _masked_distance(i * tq + row, i * tq + col)
    lam = _diff_lambda(lq1_ref, lk1_ref, lq2_ref, lk2_ref, lam_init)

    def scores(h, mp, k0, bias):
        hm = 2 * h + mp
        q = q_scr[:, hm * HEAD_DIM:(hm + 1) * HEAD_DIM]
        k = k_scr[pl.ds(k0, tq), hm * HEAD_DIM:(hm + 1) * HEAD_DIM]
        return lax.dot_general(q, k, (((1,), (1,)), ((), ())),
                               preferred_element_type=F32) * scale + bias

    for h in range(N_HEADS):
        neg_slope = -_alibi_slope(h)
        d0 = pl.multiple_of(i * tq, tq)
        bias_diag = neg_slope * dist_diag
        for mp in range(2):
            s = scores(h, mp, d0, bias_diag)
            s_scr[mp, :, pl.ds(d0, tq)] = s
            m_scr[mp] = _fold_lanes(s, jnp.maximum)
        base = neg_slope * rel

        def pass1(j, carry, h=h, base=base, neg_slope=neg_slope):
            k0 = pl.multiple_of(j * tq, tq)
            bias = base + neg_slope * ((i - j) * tq).astype(F32)
            for mp in range(2):
                s = scores(h, mp, k0, bias)
                s_scr[mp, :, pl.ds(k0, tq)] = s
                m_scr[mp] = jnp.maximum(m_scr[mp], _fold_lanes(s, jnp.maximum))
            return carry

        lax.fori_loop(0, i, pass1, 0)
        m_row = [jnp.max(m_scr[mp], axis=-1, keepdims=True) for mp in range(2)]
        l_scr[...] = jnp.zeros(l_scr.shape, F32)
        acc_scr[...] = jnp.zeros(acc_scr.shape, F32)

        def pass2(j, carry, h=h, m_row=m_row):
            k0 = pl.multiple_of(j * tq, tq)
            v = v_scr[pl.ds(k0, tq), h * V_DIM:(h + 1) * V_DIM]
            for mp in range(2):
                p = jnp.exp(s_scr[mp, :, pl.ds(k0, tq)] - m_row[mp])
                l_scr[mp] += _fold_lanes(p, jnp.add)
                acc_scr[mp] += jnp.dot(p.astype(BF16), v, preferred_element_type=F32)
            return carry

        lax.fori_loop(0, i + 1, pass2, 0)
        l_row = [jnp.sum(l_scr[mp], axis=-1, keepdims=True) for mp in range(2)]
        o = acc_scr[0] / l_row[0] - lam * (acc_scr[1] / l_row[1])
        o = _rms(o, gsub_ref[...]) * (1.0 - lam_init)
        o_ref[:, h * V_DIM:(h + 1) * V_DIM] = o.astype(BF16)


def attn_prompt(proj, seq, batch, lam_rows, gsub_row, lam_init, tq):
    rows = proj.shape[0]
    nq = seq // tq
    lam_spec = pl.BlockSpec((1, HEAD_DIM), lambda b, i: (0, 0))
    return pl.pallas_call(
        functools.partial(_attn_prompt_kernel, lam_init=lam_init),
        grid=(batch, nq),
        in_specs=[
            pl.BlockSpec((tq, D_QK), lambda b, i: (b * nq + i, COL_Q)),
            pl.BlockSpec((seq, D_QK), lambda b, i: (b, COL_K), pipeline_mode=pl.Buffered(1)),
            pl.BlockSpec((seq, D_ATTN), lambda b, i: (b, COL_V), pipeline_mode=pl.Buffered(1)),
            lam_spec, lam_spec, lam_spec, lam_spec,
            pl.BlockSpec((1, V_DIM), lambda b, i: (0, 0)),
        ],
        out_specs=pl.BlockSpec((tq, D_ATTN), lambda b, i: (b * nq + i, 0)),
        out_shape=jax.ShapeDtypeStruct((rows, D_ATTN), BF16),
        scratch_shapes=[
            pltpu.VMEM((tq, D_QK), BF16),
            pltpu.VMEM((seq, D_QK), BF16),
            pltpu.VMEM((seq, D_ATTN), BF16),
            pltpu.VMEM((2, tq, seq), F32),
            pltpu.VMEM((2, tq, 128), F32),
            pltpu.VMEM((2, tq, 128), F32),
            pltpu.VMEM((2, tq, V_DIM), F32),
        ],
        compiler_params=_params(("arbitrary", "arbitrary")),
        name="attn_prompt",
    )(proj, proj, proj, *lam_rows, gsub_row)


def _attn_sample_kernel(q_ref, kn_ref, vn_ref, kp_ref, vp_ref, lq1_ref, lk1_ref, lq2_ref, lk2_ref,
                        gsub_ref, o_prev_ref, o_ref, q_scr, m_scr, l_scr, acc_scr, *, lam_init, past):
    del o_prev_ref
    j = pl.program_id(1)
    s_len = q_ref.shape[0]
    tk = kp_ref.shape[0] // CACHE_ROWS

    @pl.when(j == 0)
    def _():
        q_scr[...] = q_ref[...].astype(BF16)
        _attn_init(m_scr, l_scr, acc_scr)

    def v_past(h):
        halves = [vp_ref[pl.ds(c * N_HEADS + h, tk, stride=CACHE_ROWS), :] for c in range(2)]
        return jnp.concatenate(halves, axis=1).astype(BF16)

    qpos = past + lax.broadcasted_iota(jnp.int32, (s_len, tk), 0)
    kpos = j * tk + lax.broadcasted_iota(jnp.int32, (s_len, tk), 1)
    _attn_tile(
        q_scr,
        lambda hm: kp_ref[pl.ds(hm, tk, stride=CACHE_ROWS), :].astype(BF16),
        v_past,
        _masked_distance(qpos, kpos), m_scr, l_scr, acc_scr)

    @pl.when(j == pl.num_programs(1) - 1)
    def _():
        qn = past + lax.broadcasted_iota(jnp.int32, (s_len, s_len), 0)
        kn = past + lax.broadcasted_iota(jnp.int32, (s_len, s_len), 1)
        _attn_tile(
            q_scr,
            lambda hm: kn_ref[:, hm * HEAD_DIM:(hm + 1) * HEAD_DIM].astype(BF16),
            lambda h: vn_ref[:, h * V_DIM:(h + 1) * V_DIM].astype(BF16),
            _masked_distance(qn, kn), m_scr, l_scr, acc_scr)
        lam = _diff_lambda(lq1_ref, lk1_ref, lq2_ref, lk2_ref, lam_init)
        _attn_finish(o_ref, lam, gsub_ref, lam_init, m_scr, l_scr, acc_scr)


def attn_sample(proj, o_prev, row0, s_len, batch, cache_k, cache_v, layer, lam_rows, gsub_row,
                lam_init, tk):
    rows = proj.shape[0]
    depth, _, past = cache_k.shape[:3]
    rb = row0 // s_len
    lam_spec = pl.BlockSpec((1, HEAD_DIM), lambda b, j: (0, 0))
    k_rows = cache_k.reshape(depth, batch, past * CACHE_ROWS, HEAD_DIM)
    v_rows = (cache_v.reshape(depth, batch, past, N_HEADS, 2, HEAD_DIM)
              .transpose(0, 1, 2, 4, 3, 5).reshape(depth, batch, past * CACHE_ROWS, HEAD_DIM))
    cache_spec = pl.BlockSpec((None, None, tk * CACHE_ROWS, HEAD_DIM), lambda b, j: (layer, b, j, 0))
    return pl.pallas_call(
        functools.partial(_attn_sample_kernel, lam_init=lam_init, past=past),
        grid=(batch, past // tk),
        in_specs=[
            pl.BlockSpec((s_len, D_QK), lambda b, j: (rb + b, COL_Q)),
            pl.BlockSpec((s_len, D_QK), lambda b, j: (rb + b, COL_K)),
            pl.BlockSpec((s_len, D_ATTN), lambda b, j: (rb + b, COL_V)),
            cache_spec, cache_spec,
            lam_spec, lam_spec, lam_spec, lam_spec,
            pl.BlockSpec((1, V_DIM), lambda b, j: (0, 0)),
            pl.BlockSpec(memory_space=pl.ANY),
        ],
        out_specs=pl.BlockSpec((s_len, D_ATTN), lambda b, j: (rb + b, 0)),
        out_shape=jax.ShapeDtypeStruct((rows, D_ATTN), BF16),
        input_output_aliases={10: 0},
        scratch_shapes=[
            pltpu.VMEM((s_len, D_QK), BF16),
            pltpu.VMEM((2 * N_HEADS, s_len, 1), F32),
            pltpu.VMEM((2 * N_HEADS, s_len, 1), F32),
            pltpu.VMEM((2 * N_HEADS, s_len, V_DIM), F32),
        ],
        compiler_params=_params(("arbitrary", "arbitrary")),
        name="attn_sample",
    )(proj, proj, proj, k_rows, v_rows, *lam_rows, gsub_row, o_prev)


def _kv_export_kernel(k_ref, v_ref, *refs):
    ko_ref, vo_ref = refs[-2:]
    for h in range(N_HEADS):
        vo_ref[:, h, :] = v_ref[:, h * V_DIM:(h + 1) * V_DIM]
        for mp in range(2):
            hm = 2 * h + mp
            ko_ref[:, h, mp, :] = k_ref[:, hm * HEAD_DIM:(hm + 1) * HEAD_DIM]


def kv_export(proj, row0, seq, batch, depth, layer, k_prev, v_prev, tr):
    rb = row0 // tr
    nr = seq // tr
    in_specs = [
        pl.BlockSpec((tr, D_QK), lambda b, i: (rb + b * nr + i, COL_K)),
        pl.BlockSpec((tr, D_ATTN), lambda b, i: (rb + b * nr + i, COL_V)),
    ]
    args = [proj, proj]
    aliases = {}
    if k_prev is not None:
        in_specs += [pl.BlockSpec(memory_space=pl.ANY), pl.BlockSpec(memory_space=pl.ANY)]
        args += [k_prev, v_prev]
        aliases = {2: 0, 3: 1}
    return pl.pallas_call(
        _kv_export_kernel,
        grid=(batch, seq // tr),
        in_specs=in_specs,
        out_specs=[
            pl.BlockSpec((None, None, tr, N_HEADS, 2, HEAD_DIM), lambda b, i: (layer, b, i, 0, 0, 0)),
            pl.BlockSpec((None, None, tr, N_HEADS, V_DIM), lambda b, i: (layer, b, i, 0, 0)),
        ],
        out_shape=[
            jax.ShapeDtypeStruct((depth, batch, seq, N_HEADS, 2, HEAD_DIM), F32),
            jax.ShapeDtypeStruct((depth, batch, seq, N_HEADS, V_DIM), F32),
        ],
        input_output_aliases=aliases,
        compiler_params=_params(("arbitrary", "arbitrary")),
        name="kv_export",
    )(*args)


def _merge_kernel(a_ref, o_ref, wa_ref, wo_ref, ga_ref, go_ref, ba_ref, bo_ref, out_ref):
    ya = jnp.dot(a_ref[...], wa_ref[...], preferred_element_type=F32)
    yo = jnp.dot(o_ref[...], wo_ref[...], preferred_element_type=F32)
    merged = _sigmoid(ga_ref[...] + ba_ref[...]) * ya + _sigmoid(go_ref[...] + bo_ref[...]) * yo
    out_ref[...] = merged.astype(BF16)


def gated_merge(a_out, o_out, w_ssm_all, w_attn_all, proj, b_ssm_row, b_attn_row, layer, tm, tn):
    m, d_br = a_out.shape
    d_model = w_ssm_all.shape[2]
    gate0 = COL_GATE * D_QK // tn
    nj = d_model // tn
    return pl.pallas_call(
        _merge_kernel,
        grid=(m // tm, nj),
        in_specs=[
            pl.BlockSpec((tm, d_br), lambda i, j: (i, 0)),
            pl.BlockSpec((tm, d_br), lambda i, j: (i, 0)),
            pl.BlockSpec((None, d_br, tn), lambda i, j: (layer, 0, j)),
            pl.BlockSpec((None, d_br, tn), lambda i, j: (layer, 0, j)),
            pl.BlockSpec((tm, tn), lambda i, j: (i, gate0 + j)),
            pl.BlockSpec((tm, tn), lambda i, j: (i, gate0 + nj + j)),
            pl.BlockSpec((1, tn), lambda i, j: (0, j)),
            pl.BlockSpec((1, tn), lambda i, j: (0, j)),
        ],
        out_specs=pl.BlockSpec((tm, tn), lambda i, j: (i, j)),
        out_shape=jax.ShapeDtypeStruct((m, d_model), BF16),
        compiler_params=_params(("parallel", "arbitrary")),
        name="gated_merge",
    )(a_out, o_out, w_ssm_all, w_attn_all, proj, proj, b_ssm_row, b_attn_row)


def _out_proj_kernel(a_ref, w_ref, x_ref, g_ref, o_ref, *, tn):
    n = o_ref.shape[1]
    a = a_ref[...]
    ssq = jnp.zeros((a.shape[0], 1), F32)
    for c in range(n // tn):
        y = jnp.dot(a, w_ref[:, c * tn:(c + 1) * tn], preferred_element_type=F32)
        ssq = ssq + jnp.sum(y * y, axis=-1, keepdims=True)
        o_ref[:, c * tn:(c + 1) * tn] = y
    inv = lax.rsqrt(ssq / n + EPS)
    o_ref[...] = x_ref[...] + o_ref[...] * inv * g_ref[...]


def out_proj_residual(merged, w_all, x, g_row, layer, tm, tn):
    m, d = x.shape
    k = merged.shape[1]
    return pl.pallas_call(
        functools.partial(_out_proj_kernel, tn=tn),
        grid=(m // tm,),
        in_specs=[
            pl.BlockSpec((tm, k), lambda i: (i, 0)),
            pl.BlockSpec((None, k, d), lambda i: (layer, 0, 0)),
            pl.BlockSpec((tm, d), lambda i: (i, 0)),
            pl.BlockSpec((1, d), lambda i: (0, 0)),
        ],
        out_specs=pl.BlockSpec((tm, d), lambda i: (i, 0)),
        out_shape=jax.ShapeDtypeStruct((m, d), F32),
        compiler_params=_params(("parallel",)),
        name="out_proj_residual",
    )(merged, w_all, x, g_row)


def _ffn_kernel(x_ref, gpre_ref, wg_ref, wu_ref, wd_ref, gpost_ref, o_ref, h_scr):
    j = pl.program_id(1)

    @pl.when(j == 0)
    def _():
        h_scr[...] = _rms(x_ref[...], gpre_ref[...]).astype(BF16)
        o_ref[...] = jnp.zeros(o_ref.shape, F32)

    h = h_scr[...]
    gate = jnp.dot(h, wg_ref[...], preferred_element_type=F32)
    up = jnp.dot(h, wu_ref[...], preferred_element_type=F32)
    act = (gate * _sigmoid(gate) * up).astype(BF16)
    o_ref[...] += jnp.dot(act, wd_ref[...], preferred_element_type=F32)

    @pl.when(j == pl.num_programs(1) - 1)
    def _():
        o_ref[...] = x_ref[...] + _rms(o_ref[...], gpost_ref[...])


def ffn_residual(x, gpre_row, wg_all, wu_all, wd_all, gpost_row, layer, tm, tf):
    m, d = x.shape
    f = wg_all.shape[2]
    return pl.pallas_call(
        _ffn_kernel,
        grid=(m // tm, f // tf),
        in_specs=[
            pl.BlockSpec((tm, d), lambda i, j: (i, 0)),
            pl.BlockSpec((1, d), lambda i, j: (0, 0)),
            pl.BlockSpec((None, d, tf), lambda i, j: (layer, 0, j)),
            pl.BlockSpec((None, d, tf), lambda i, j: (layer, 0, j)),
            pl.BlockSpec((None, tf, d), lambda i, j: (layer, j, 0)),
            pl.BlockSpec((1, d), lambda i, j: (0, 0)),
        ],
        out_specs=pl.BlockSpec((tm, d), lambda i, j: (i, 0)),
        out_shape=jax.ShapeDtypeStruct((m, d), F32),
        scratch_shapes=[pltpu.VMEM((tm, d), BF16)],
        compiler_params=_params(("parallel", "arbitrary")),
        name="swiglu_residual",
    )(x, gpre_row, wg_all, wu_all, wd_all, gpost_row)


def _tile(n, want):
    t = min(n, want)
    assert n % t == 0, (n, want)
    return t


def kernel(x_prompt, x_sample, cache_k, cache_v, state_ssm_re, state_ssm_im, g_pre_mix, w_in, b_gate, ssm_a_re, ssm_a_im, ssm_log_dt, ssm_b_re, ssm_b_im, ssm_c_re, ssm_c_im, ssm_d, w_glu, b_glu, lam_q1, lam_k1, lam_q2, lam_k2, g_sub, w_br_ssm, w_br_attn, w_out, g_post_mix, g_pre_ffn, w_ffn_gate, w_ffn_up, w_ffn_down, g_post_ffn):
    batch, seq, d_model = x_prompt.shape
    dec_batch, dec_seq, _ = x_sample.shape
    depth = w_in.shape[0]
    past = cache_k.shape[2]
    n_groups, state_dim = ssm_a_re.shape[1:]
    n_state = n_groups * state_dim
    assert ssm_d.shape[1] == D_SSM and w_in.shape[2] == COL_GATE * D_QK + N_BRANCH * d_model
    assert w_in.shape[2] % D_QK == 0
    rows_p = seq * batch
    rows_s = dec_seq * dec_batch

    x = jnp.concatenate([x_prompt.reshape(rows_p, d_model),
                         x_sample.reshape(rows_s, d_model)], axis=0)

    w_in_b = w_in.astype(BF16)
    w_glu_b = w_glu.astype(BF16)
    w_br_ssm_b = w_br_ssm.astype(BF16)
    w_br_attn_b = w_br_attn.astype(BF16)
    w_out_b = w_out.astype(BF16)
    w_gate_b = w_ffn_gate.astype(BF16)
    w_up_b = w_ffn_up.astype(BF16)
    w_down_b = w_ffn_down.astype(BF16)

    tm = _tile(rows_s, 1024)
    assert rows_p % tm == 0
    ssm_rows = 256
    zeros_state = jnp.zeros((batch, n_state), F32)

    hr_p, hi_p, hr_s, hi_s = [], [], [], []
    k_p = v_p = k_s = v_s = None
    for l in range(depth):
        lam_init = _lambda_init(l)
        proj = norm_matmul(x, g_pre_mix[l][None], w_in_b, l, tm, _tile(w_in.shape[2], 512))

        ab_re, ab_im, bc, cc = ssm_prep(ssm_a_re[l], ssm_a_im[l], ssm_log_dt[l], ssm_b_re[l],
                                        ssm_b_im[l], ssm_c_re[l], ssm_c_im[l])
        d_row = ssm_d[l][None]
        bg_row = b_glu[l][None]
        a_p, hrp, hip = ssm_branch(proj, 0, seq, batch, ssm_rows // batch, zeros_state, zeros_state,
                                   ab_re, ab_im, bc, cc, d_row, w_glu_b, l, bg_row)
        a_s, hrs, his = ssm_branch(proj, rows_p, dec_seq, dec_batch, ssm_rows // dec_batch,
                                   state_ssm_re[l].reshape(dec_batch, n_state),
                                   state_ssm_im[l].reshape(dec_batch, n_state),
                                   ab_re, ab_im, bc, cc, d_row, w_glu_b, l, bg_row)
        a_out = jnp.concatenate([a_p, a_s], axis=0)

        lam_rows = (lam_q1[l][None], lam_k1[l][None], lam_q2[l][None], lam_k2[l][None])
        gsub_row = g_sub[l][None]
        o_out = attn_prompt(proj, seq, batch, lam_rows, gsub_row, lam_init, _tile(seq, 512))
        o_out = attn_sample(proj, o_out, rows_p, dec_seq, dec_batch, cache_k, cache_v, l, lam_rows,
                            gsub_row, lam_init, _tile(past, 1024))
        k_p, v_p = kv_export(proj, 0, seq, batch, depth, l, k_p, v_p, _tile(seq, 512))
        k_s, v_s = kv_export(proj, rows_p, dec_seq, dec_batch, depth, l, k_s, v_s, dec_seq)

        merged = gated_merge(a_out, o_out, w_br_ssm_b, w_br_attn_b, proj, b_gate[l, 0][None],
                             b_gate[l, 1][None], l, tm, _tile(d_model, 512))
        x = out_proj_residual(merged, w_out_b, x, g_post_mix[l][None], l, _tile(rows_s, 512),
                              _tile(d_model, 512))
        x = ffn_residual(x, g_pre_ffn[l][None], w_gate_b, w_up_b, w_down_b, g_post_ffn[l][None],
                         l, _tile(rows_s, 512), _tile(w_ffn_gate.shape[2], 512))

        hr_p.append(hrp.reshape(batch, n_groups, state_dim))
        hi_p.append(hip.reshape(batch, n_groups, state_dim))
        hr_s.append(hrs.reshape(dec_batch, n_groups, state_dim))
        hi_s.append(his.reshape(dec_batch, n_groups, state_dim))

    y_p = x[:rows_p].reshape(batch, seq, d_model)
    y_s = x[rows_p:].reshape(dec_batch, dec_seq, d_model)
    return (y_p, y_s, k_p, v_p, jnp.stack(hr_p), jnp.stack(hi_p),
            k_s, v_s, jnp.stack(hr_s), jnp.stack(hi_s))
```

```python
import functools
import math

import jax
import jax.numpy as jnp
from jax import lax
from jax.experimental import pallas as pl
from jax.experimental.pallas import tpu as pltpu

F32 = jnp.float32
BF16 = jnp.bfloat16

CHUNK = 64
SSM_GROUP = 16
STATE_DIM = 64
N_HEADS = 4
HEAD_DIM = 128
V_DIM = 2 * HEAD_DIM
N_BRANCH = 2
EPS = 1e-6
D_SSM = 1024
D_QK = N_HEADS * 2 * HEAD_DIM
D_ATTN = N_HEADS * V_DIM
assert D_SSM == D_QK == D_ATTN
COL_Q, COL_K, COL_V, COL_GATE = 1, 2, 3, 4
CACHE_ROWS = D_QK // HEAD_DIM

V7X_MXU_DIM = 256
V7X_SUBLANES = 8
LANES = 128
V7X_VMEM_LIMIT = 56 * 1024 * 1024

SSM_KTILE = V7X_MXU_DIM
SSM_GROUPS_PER_TILE = SSM_KTILE // SSM_GROUP
SSM_STATE_PER_TILE = SSM_GROUPS_PER_TILE * STATE_DIM


def _params(sem, vmem=V7X_VMEM_LIMIT):
    return pltpu.CompilerParams(dimension_semantics=sem, vmem_limit_bytes=vmem)


def _rms(xf, g):
    ms = jnp.mean(xf * xf, axis=-1, keepdims=True)
    return xf * lax.rsqrt(ms + EPS) * g


def _sigmoid(x):
    return 1.0 / (1.0 + jnp.exp(-x))


def _gelu_tanh(x):
    c = math.sqrt(2.0 / math.pi)
    return x * (0.5 * (1.0 + jnp.tanh(c * (x + 0.044715 * (x * x * x)))))


def _lambda_init(layer):
    return 0.8 - 0.6 * math.exp(-0.3 * layer)


def _alibi_slope(h):
    return 2.0 ** (-8.0 * (h + 1) / N_HEADS)


def _norm_matmul_kernel(x_ref, g_ref, w_ref, bgate_ref, o_ref, gate_ref, h_scr, *, n_main):
    j = pl.program_id(1)

    @pl.when(j == 0)
    def _():
        h_scr[...] = _rms(x_ref[...], g_ref[...]).astype(BF16)

    @pl.when(j < n_main)
    def _():
        o_ref[...] = jnp.dot(h_scr[...], w_ref[...], preferred_element_type=F32)

    @pl.when(j >= n_main)
    def _():
        logits = jnp.dot(h_scr[...], w_ref[...], preferred_element_type=F32)
        gate_ref[...] = _sigmoid(logits + bgate_ref[...]).astype(BF16)


def norm_matmul(x, g, w_all, b_gate_row, layer, tm, tn):
    m, d = x.shape
    n = w_all.shape[2]
    n_main = COL_GATE * D_QK // tn
    n_gate = b_gate_row.shape[1] // tn
    assert n_main + n_gate == n // tn
    return pl.pallas_call(
        functools.partial(_norm_matmul_kernel, n_main=n_main),
        grid=(m // tm, n // tn),
        in_specs=[
            pl.BlockSpec((tm, d), lambda i, j: (i, 0)),
            pl.BlockSpec((1, d), lambda i, j: (0, 0)),
            pl.BlockSpec((None, d, tn), lambda i, j: (layer, 0, j)),
            pl.BlockSpec((1, tn), lambda i, j: (0, jnp.maximum(j - n_main, 0))),
        ],
        out_specs=[
            pl.BlockSpec((tm, tn), lambda i, j: (i, jnp.minimum(j, n_main - 1))),
            pl.BlockSpec((tm, tn), lambda i, j: (i, jnp.maximum(j - n_main, 0))),
        ],
        out_shape=[jax.ShapeDtypeStruct((m, n_main * tn), F32),
                   jax.ShapeDtypeStruct((m, n_gate * tn), BF16)],
        scratch_shapes=[pltpu.VMEM((tm, d), BF16)],
        compiler_params=_params(("arbitrary", "arbitrary")),
        name="norm_in_proj",
    )(x, g, w_all, b_gate_row)


def _ssm_prep_kernel(are_ref, aim_ref, ldt_ref, bre_ref, bim_ref, cre_ref, cim_ref,
                     abre_ref, abim_ref, bc_ref, cc_ref):
    a_re = are_ref[...]
    a_im = aim_ref[...]
    dt = jnp.exp(ldt_ref[...])
    mag = jnp.exp(dt * a_re)
    ab_re = mag * jnp.cos(dt * a_im)
    ab_im = mag * jnp.sin(dt * a_im)
    n_re = ab_re - 1.0
    n_im = ab_im
    den = a_re * a_re + a_im * a_im
    c_re = (n_re * a_re + n_im * a_im) / den
    c_im = (n_im * a_re - n_re * a_im) / den
    abre_ref[...] = ab_re
    abim_ref[...] = ab_im
    b_re = bre_ref[...]
    b_im = bim_ref[...]
    ns = b_re.shape[1]
    bc_ref[:, :ns] = (c_re * b_re - c_im * b_im).astype(BF16)
    bc_ref[:, ns:] = (c_re * b_im + c_im * b_re).astype(BF16)
    cc_ref[:ns, :] = cre_ref[...].astype(BF16)
    cc_ref[ns:, :] = (-cim_ref[...]).astype(BF16)


def ssm_prep(a_re, a_im, log_dt, b_re, b_im, c_re, c_im):
    g, p = a_re.shape
    kt = g // SSM_GROUPS_PER_TILE
    ns = SSM_STATE_PER_TILE
    eye = jnp.eye(SSM_GROUPS_PER_TILE, dtype=F32)

    def place_b(b):
        bt = b.reshape(kt, SSM_GROUPS_PER_TILE, p, SSM_GROUP).transpose(0, 1, 3, 2)
        return jnp.einsum('kgcp,gh->kgchp', bt, eye).reshape(kt, SSM_KTILE, ns)

    def place_c(c):
        ct = c.reshape(kt, SSM_GROUPS_PER_TILE, SSM_GROUP, p).transpose(0, 1, 3, 2)
        return jnp.einsum('kgpc,gh->kgphc', ct, eye).reshape(kt, ns, SSM_KTILE)

    row = lambda a: a.reshape(1, g * p)
    ldt = row(jnp.broadcast_to(log_dt[:, None], (g, p)))
    vec = pl.BlockSpec((1, ns), lambda k: (0, k))
    bspec = pl.BlockSpec((None, SSM_KTILE, ns), lambda k: (k, 0, 0))
    cspec = pl.BlockSpec((None, ns, SSM_KTILE), lambda k: (k, 0, 0))
    return pl.pallas_call(
        _ssm_prep_kernel,
        grid=(kt,),
        in_specs=[vec, vec, vec, bspec, bspec, cspec, cspec],
        out_specs=[vec, vec,
                   pl.BlockSpec((None, SSM_KTILE, 2 * ns), lambda k: (k, 0, 0)),
                   pl.BlockSpec((None, 2 * ns, SSM_KTILE), lambda k: (k, 0, 0))],
        out_shape=[jax.ShapeDtypeStruct((1, g * p), F32),
                   jax.ShapeDtypeStruct((1, g * p), F32),
                   jax.ShapeDtypeStruct((kt, SSM_KTILE, 2 * ns), BF16),
                   jax.ShapeDtypeStruct((kt, 2 * ns, SSM_KTILE), BF16)],
        compiler_params=_params(("arbitrary",)),
        name="ssm_discretise",
    )(row(a_re), row(a_im), ldt, place_b(b_re), place_b(b_im), place_c(c_re), place_c(c_im))


SSM_SCAN_LANES = 256


def _ssm_kernel(*refs, nb, n_kt):
    u_refs = refs[:nb]
    (h0r_ref, h0i_ref, ar_ref, ai_ref, bc_ref, cc_ref, d_ref, wg_ref, bg_ref,
     out_ref, hr_ref, hi_ref, io_scr, hs_scr) = refs[nb:]
    ns = SSM_STATE_PER_TILE
    tc, d_ssm = u_refs[0].shape
    rows = nb * tc
    n_lt = d_ssm // LANES

    @pl.when(pl.program_id(0) == 0)
    def _():
        hr_ref[...] = h0r_ref[...]
        hi_ref[...] = h0i_ref[...]

    for b in range(nb):
        for j in range(n_lt):
            io_scr[j, pl.ds(b, tc, stride=nb), :] = u_refs[b][:, j * LANES:(j + 1) * LANES]
    u = jnp.concatenate([io_scr[j] for j in range(n_lt)], axis=1)
    ub = u.astype(BF16)
    def input_map(kt):
        hs_scr[:, kt * 2 * ns:(kt + 1) * 2 * ns] = jnp.dot(
            ub[:, kt * SSM_KTILE:(kt + 1) * SSM_KTILE], bc_ref[kt], preferred_element_type=F32)

    rpi = hr_ref.shape[0]
    two_step = rpi != nb
    n_iter = rows // rpi
    w = SSM_SCAN_LANES
    low = lax.broadcasted_iota(jnp.int32, (rpi, w), 0) < nb

    def recurrence(kt):
        for s in range(ns // w):
            re0 = kt * 2 * ns + s * w
            im0 = re0 + ns
            a0 = kt * ns + s * w
            ar = jnp.broadcast_to(ar_ref[:, a0:a0 + w], (rpi, w))
            ai = jnp.broadcast_to(ai_ref[:, a0:a0 + w], (rpi, w))
            hr = hr_ref[:, a0:a0 + w]
            hi = hi_ref[:, a0:a0 + w]
            for it in range(n_iter):
                r0 = it * rpi
                br = hs_scr[r0:r0 + rpi, re0:re0 + w]
                bi = hs_scr[r0:r0 + rpi, im0:im0 + w]
                nr = ar * hr - ai * hi + br
                ni = ar * hi + ai * hr + bi
                if two_step:
                    sr = pltpu.roll(nr, nb, 0)
                    si = pltpu.roll(ni, nb, 0)
                    mr = ar * sr - ai * si + br
                    mi = ar * si + ai * sr + bi
                    hs_scr[r0:r0 + rpi, re0:re0 + w] = jnp.where(low, nr, mr)
                    hs_scr[r0:r0 + rpi, im0:im0 + w] = jnp.where(low, ni, mi)
                    hr = pltpu.roll(mr, nb, 0)
                    hi = pltpu.roll(mi, nb, 0)
                else:
                    hs_scr[r0:r0 + rpi, re0:re0 + w] = nr
                    hs_scr[r0:r0 + rpi, im0:im0 + w] = ni
                    hr, hi = nr, ni
            hr_ref[:, a0:a0 + w] = hr
            hi_ref[:, a0:a0 + w] = hi

    cols = []
    input_map(0)
    for kt in range(n_kt):
        if kt + 1 < n_kt:
            input_map(kt + 1)
        recurrence(kt)
        hk = hs_scr[:, kt * 2 * ns:(kt + 1) * 2 * ns].astype(BF16)
        cols.append(jnp.dot(hk, cc_ref[kt], preferred_element_type=F32))
    y = jnp.concatenate(cols, axis=1) + d_ref[...] * u
    z = _gelu_tanh(y)
    gate = _sigmoid(jnp.dot(z.astype(BF16), wg_ref[...], preferred_element_type=F32) + bg_ref[...])
    out = z * gate
    for j in range(n_lt):
        io_scr[j] = out[:, j * LANES:(j + 1) * LANES]
    for b in range(nb):
        out_ref[b] = jnp.concatenate(
            [io_scr[j, pl.ds(b, tc, stride=nb), :] for j in range(n_lt)], axis=1).astype(BF16)


def ssm_branch(proj, row0, seq, nb, tc, h0_re, h0_im, ab_re, ab_im, bc, cc, d_row, w_glu_all,
               layer, b_glu_row):
    d_ssm = d_row.shape[1]
    n_kt = d_ssm // SSM_KTILE
    n_state = ab_re.shape[1]
    full = lambda a: pl.BlockSpec(a.shape, lambda c: (0,) * a.ndim)
    srows = max(nb, V7X_SUBLANES)
    assert srows == nb or srows == 2 * nb
    if srows != nb:
        h0_re = jnp.concatenate([h0_re, h0_re], axis=0)
        h0_im = jnp.concatenate([h0_im, h0_im], axis=0)
    n_chunks = seq // tc
    rb0 = row0 // tc
    u_specs = [pl.BlockSpec((tc, d_ssm), lambda c, b=b: (rb0 + b * n_chunks + c, 0))
               for b in range(nb)]
    a_out, h_re, h_im = pl.pallas_call(
        functools.partial(_ssm_kernel, nb=nb, n_kt=n_kt),
        grid=(n_chunks,),
        in_specs=u_specs + [
            full(h0_re), full(h0_im), full(ab_re), full(ab_im), full(bc), full(cc), full(d_row),
            pl.BlockSpec((None, d_ssm, d_ssm), lambda c: (layer, 0, 0)),
            full(b_glu_row),
        ],
        out_specs=[
            pl.BlockSpec((nb, tc, d_ssm), lambda c: (0, c, 0)),
            pl.BlockSpec((srows, n_state), lambda c: (0, 0)),
            pl.BlockSpec((srows, n_state), lambda c: (0, 0)),
        ],
        out_shape=[
            jax.ShapeDtypeStruct((nb, seq, d_ssm), BF16),
            jax.ShapeDtypeStruct((srows, n_state), F32),
            jax.ShapeDtypeStruct((srows, n_state), F32),
        ],
        scratch_shapes=[pltpu.VMEM((d_ssm // LANES, nb * tc, LANES), F32),
                        pltpu.VMEM((nb * tc, 2 * n_state), F32)],
        compiler_params=_params(("arbitrary",)),
        name="ssm_scan",
    )(*([proj] * nb), h0_re, h0_im, ab_re, ab_im, bc, cc, d_row, w_glu_all, b_glu_row)
    return a_out.reshape(nb * seq, d_ssm), h_re[:nb], h_im[:nb]


def _diff_lambda(lq1_ref, lk1_ref, lq2_ref, lk2_ref, lam_init):
    s1 = jnp.sum(lq1_ref[...] * lk1_ref[...], axis=-1, keepdims=True)
    s2 = jnp.sum(lq2_ref[...] * lk2_ref[...], axis=-1, keepdims=True)
    return jnp.exp(s1) - jnp.exp(s2) + lam_init


def _masked_distance(qpos, kpos):
    dist = jnp.abs(qpos - kpos).astype(F32)
    shift = CHUNK.bit_length() - 1
    assert CHUNK == 1 << shift
    allowed = jnp.right_shift(kpos, shift) <= jnp.right_shift(qpos, shift)
    return jnp.where(allowed, dist, jnp.inf)


def _softmax_step(hm, s, v, m_scr, l_scr, acc_scr):
    m_old = m_scr[hm]
    m_new = jnp.maximum(m_old, jnp.max(s, axis=-1, keepdims=True))
    alpha = jnp.exp(m_old - m_new)
    p = jnp.exp(s - m_new)
    l_scr[hm] = alpha * l_scr[hm] + jnp.sum(p, axis=-1, keepdims=True)
    acc_scr[hm] = alpha * acc_scr[hm] + jnp.dot(p.astype(BF16), v, preferred_element_type=F32)
    m_scr[hm] = m_new


def _attn_tile(q_scr, k_tile, v_tile, distm, m_scr, l_scr, acc_scr):
    scale = HEAD_DIM ** -0.5
    for h in range(N_HEADS):
        bias = (-_alibi_slope(h)) * distm
        v = v_tile(h)
        for mp in range(2):
            hm = 2 * h + mp
            q = q_scr[:, hm * HEAD_DIM:(hm + 1) * HEAD_DIM]
            s = lax.dot_general(q, k_tile(hm), (((1,), (1,)), ((), ())),
                                preferred_element_type=F32) * scale + bias
            _softmax_step(hm, s, v, m_scr, l_scr, acc_scr)


def _attn_init(m_scr, l_scr, acc_scr):
    m_scr[...] = jnp.full(m_scr.shape, -jnp.inf, F32)
    l_scr[...] = jnp.zeros(l_scr.shape, F32)
    acc_scr[...] = jnp.zeros(acc_scr.shape, F32)


def _attn_finish(o_ref, lam, gsub_ref, lam_init, m_scr, l_scr, acc_scr):
    for h in range(N_HEADS):
        o = acc_scr[2 * h] / l_scr[2 * h] - lam * (acc_scr[2 * h + 1] / l_scr[2 * h + 1])
        o = _rms(o, gsub_ref[...]) * (1.0 - lam_init)
        o_ref[:, h * V_DIM:(h + 1) * V_DIM] = o.astype(BF16)


def _fold_lanes(x, op):
    out = x[:, :LANES]
    for c in range(1, x.shape[1] // LANES):
        out = op(out, x[:, c * LANES:(c + 1) * LANES])
    return out


def _attn_prompt_kernel(q_ref, k_ref, v_ref, lq1_ref, lk1_ref, lq2_ref, lk2_ref, gsub_ref,
                        o_ref, q_scr, k_scr, v_scr, s_scr, m_scr, l_scr, acc_scr, *, lam_init):
    i = pl.program_id(1)
    tq = q_ref.shape[0]
    scale = HEAD_DIM ** -0.5

    @pl.when(i == 0)
    def _():
        k_scr[...] = k_ref[...].astype(BF16)
        v_scr[...] = v_ref[...].astype(BF16)

    q_scr[...] = q_ref[...].astype(BF16)
    row = lax.broadcasted_iota(jnp.int32, (tq, tq), 0)
    col = lax.broadcasted_iota(jnp.int32, (tq, tq), 1)
    rel = (row - col).astype(F32)
    dist_diag = _masked_distance(i * tq + row, i * tq + col)
    lam = _diff_lambda(lq1_ref, lk1_ref, lq2_ref, lk2_ref, lam_init)

    def scores(h, mp, k0, bias):
        hm = 2 * h + mp
        q = q_scr[:, hm * HEAD_DIM:(hm + 1) * HEAD_DIM]
        k = k_scr[pl.ds(k0, tq), hm * HEAD_DIM:(hm + 1) * HEAD_DIM]
        return lax.dot_general(q, k, (((1,), (1,)), ((), ())),
                               preferred_element_type=F32) * scale + bias

    for h in range(N_HEADS):
        neg_slope = -_alibi_slope(h)
        d0 = pl.multiple_of(i * tq, tq)
        bias_diag = neg_slope * dist_diag
        for mp in range(2):
            s = scores(h, mp, d0, bias_diag)
            s_scr[mp, :, pl.ds(d0, tq)] = s
            m_scr[mp] = _fold_lanes(s, jnp.maximum)
        base = neg_slope * rel

        def pass1(j, carry, h=h, base=base, neg_slope=neg_slope):
            k0 = pl.multiple_of(j * tq, tq)
            bias = base + neg_slope * ((i - j) * tq).astype(F32)
            for mp in range(2):
                s = scores(h, mp, k0, bias)
                s_scr[mp, :, pl.ds(k0, tq)] = s
                m_scr[mp] = jnp.maximum(m_scr[mp], _fold_lanes(s, jnp.maximum))
            return carry

        lax.fori_loop(0, i, pass1, 0)
        m_row = [jnp.max(m_scr[mp], axis=-1, keepdims=True) for mp in range(2)]
        l_scr[...] = jnp.zeros(l_scr.shape, F32)
        acc_scr[...] = jnp.zeros(acc_scr.shape, F32)

        def pass2(j, carry, h=h, m_row=m_row):
            k0 = pl.multiple_of(j * tq, tq)
            v = v_scr[pl.ds(k0, tq), h * V_DIM:(h + 1) * V_DIM]
            for mp in range(2):
                p = jnp.exp(s_scr[mp, :, pl.ds(k0, tq)] - m_row[mp])
                l_scr[mp] += _fold_lanes(p, jnp.add)
                acc_scr[mp] += jnp.dot(p.astype(BF16), v, preferred_element_type=F32)
            return carry

        lax.fori_loop(0, i + 1, pass2, 0)
        l_row = [jnp.sum(l_scr[mp], axis=-1, keepdims=True) for mp in range(2)]
        o = acc_scr[0] / l_row[0] - lam * (acc_scr[1] / l_row[1])
        o = _rms(o, gsub_ref[...]) * (1.0 - lam_init)
        o_ref[:, h * V_DIM:(h + 1) * V_DIM] = o.astype(BF16)


def attn_prompt(proj, seq, batch, lam_rows, gsub_row, lam_init, tq):
    rows = proj.shape[0]
    nq = seq // tq
    lam_spec = pl.BlockSpec((1, HEAD_DIM), lambda b, i: (0, 0))
    return pl.pallas_call(
        functools.partial(_attn_prompt_kernel, lam_init=lam_init),
        grid=(batch, nq),
        in_specs=[
            pl.BlockSpec((tq, D_QK), lambda b, i: (b * nq + i, COL_Q)),
            pl.BlockSpec((seq, D_QK), lambda b, i: (b, COL_K), pipeline_mode=pl.Buffered(1)),
            pl.BlockSpec((seq, D_ATTN), lambda b, i: (b, COL_V), pipeline_mode=pl.Buffered(1)),
            lam_spec, lam_spec, lam_spec, lam_spec,
            pl.BlockSpec((1, V_DIM), lambda b, i: (0, 0)),
        ],
        out_specs=pl.BlockSpec((tq, D_ATTN), lambda b, i: (b * nq + i, 0)),
        out_shape=jax.ShapeDtypeStruct((rows, D_ATTN), BF16),
        scratch_shapes=[
            pltpu.VMEM((tq, D_QK), BF16),
            pltpu.VMEM((seq, D_QK), BF16),
            pltpu.VMEM((seq, D_ATTN), BF16),
            pltpu.VMEM((2, tq, seq), F32),
            pltpu.VMEM((2, tq, 128), F32),
            pltpu.VMEM((2, tq, 128), F32),
            pltpu.VMEM((2, tq, V_DIM), F32),
        ],
        compiler_params=_params(("arbitrary", "arbitrary")),
        name="attn_prompt",
    )(proj, proj, proj, *lam_rows, gsub_row)


def _attn_sample_kernel(q_ref, kn_ref, vn_ref, kp_ref, vp_ref, lq1_ref, lk1_ref, lq2_ref, lk2_ref,
                        gsub_ref, o_prev_ref, o_ref, q_scr, m_scr, l_scr, acc_scr, *, lam_init, past):
    del o_prev_ref
    j = pl.program_id(1)
    s_len = q_ref.shape[0]
    tk = kp_ref.shape[0] // CACHE_ROWS

    @pl.when(j == 0)
    def _():
        q_scr[...] = q_ref[...].astype(BF16)
        _attn_init(m_scr, l_scr, acc_scr)

    def v_past(h):
        halves = [vp_ref[pl.ds(c * N_HEADS + h, tk, stride=CACHE_ROWS), :] for c in range(2)]
        return jnp.concatenate(halves, axis=1).astype(BF16)

    qpos = past + lax.broadcasted_iota(jnp.int32, (s_len, tk), 0)
    kpos = j * tk + lax.broadcasted_iota(jnp.int32, (s_len, tk), 1)
    _attn_tile(
        q_scr,
        lambda hm: kp_ref[pl.ds(hm, tk, stride=CACHE_ROWS), :].astype(BF16),
        v_past,
        _masked_distance(qpos, kpos), m_scr, l_scr, acc_scr)

    @pl.when(j == pl.num_programs(1) - 1)
    def _():
        qn = past + lax.broadcasted_iota(jnp.int32, (s_len, s_len), 0)
        kn = past + lax.broadcasted_iota(jnp.int32, (s_len, s_len), 1)
        _attn_tile(
            q_scr,
            lambda hm: kn_ref[:, hm * HEAD_DIM:(hm + 1) * HEAD_DIM].astype(BF16),
            lambda h: vn_ref[:, h * V_DIM:(h + 1) * V_DIM].astype(BF16),
            _masked_distance(qn, kn), m_scr, l_scr, acc_scr)
        lam = _diff_lambda(lq1_ref, lk1_ref, lq2_ref, lk2_ref, lam_init)
        _attn_finish(o_ref, lam, gsub_ref, lam_init, m_scr, l_scr, acc_scr)


def attn_sample(proj, o_prev, row0, s_len, batch, cache_k, cache_v, layer, lam_rows, gsub_row,
                lam_init, tk):
    rows = proj.shape[0]
    depth, _, past = cache_k.shape[:3]
    rb = row0 // s_len
    lam_spec = pl.BlockSpec((1, HEAD_DIM), lambda b, j: (0, 0))
    k_rows = cache_k.reshape(depth, batch, past * CACHE_ROWS, HEAD_DIM)
    v_rows = (cache_v.reshape(depth, batch, past, N_HEADS, 2, HEAD_DIM)
              .transpose(0, 1, 2, 4, 3, 5).reshape(depth, batch, past * CACHE_ROWS, HEAD_DIM))
    cache_spec = pl.BlockSpec((None, None, tk * CACHE_ROWS, HEAD_DIM), lambda b, j: (layer, b, j, 0))
    return pl.pallas_call(
        functools.partial(_attn_sample_kernel, lam_init=lam_init, past=past),
        grid=(batch, past // tk),
        in_specs=[
            pl.BlockSpec((s_len, D_QK), lambda b, j: (rb + b, COL_Q)),
            pl.BlockSpec((s_len, D_QK), lambda b, j: (rb + b, COL_K)),
            pl.BlockSpec((s_len, D_ATTN), lambda b, j: (rb + b, COL_V)),
            cache_spec, cache_spec,
            lam_spec, lam_spec, lam_spec, lam_spec,
            pl.BlockSpec((1, V_DIM), lambda b, j: (0, 0)),
            pl.BlockSpec(memory_space=pl.ANY),
        ],
        out_specs=pl.BlockSpec((s_len, D_ATTN), lambda b, j: (rb + b, 0)),
        out_shape=jax.ShapeDtypeStruct((rows, D_ATTN), BF16),
        input_output_aliases={10: 0},
        scratch_shapes=[
            pltpu.VMEM((s_len, D_QK), BF16),
            pltpu.VMEM((2 * N_HEADS, s_len, 1), F32),
            pltpu.VMEM((2 * N_HEADS, s_len, 1), F32),
            pltpu.VMEM((2 * N_HEADS, s_len, V_DIM), F32),
        ],
        compiler_params=_params(("arbitrary", "arbitrary")),
        name="attn_sample",
    )(proj, proj, proj, k_rows, v_rows, *lam_rows, gsub_row, o_prev)


def _kv_export_kernel(k_ref, v_ref, *refs):
    ko_ref, vo_ref = refs[-2:]
    for h in range(N_HEADS):
        vo_ref[:, h, :] = v_ref[:, h * V_DIM:(h + 1) * V_DIM]
        for mp in range(2):
            hm = 2 * h + mp
            ko_ref[:, h, mp, :] = k_ref[:, hm * HEAD_DIM:(hm + 1) * HEAD_DIM]


def kv_export(proj, row0, seq, batch, depth, layer, k_prev, v_prev, tr):
    rb = row0 // tr
    nr = seq // tr
    in_specs = [
        pl.BlockSpec((tr, D_QK), lambda b, i: (rb + b * nr + i, COL_K)),
        pl.BlockSpec((tr, D_ATTN), lambda b, i: (rb + b * nr + i, COL_V)),
    ]
    args = [proj, proj]
    aliases = {}
    if k_prev is not None:
        in_specs += [pl.BlockSpec(memory_space=pl.ANY), pl.BlockSpec(memory_space=pl.ANY)]
        args += [k_prev, v_prev]
        aliases = {2: 0, 3: 1}
    return pl.pallas_call(
        _kv_export_kernel,
        grid=(batch, seq // tr),
        in_specs=in_specs,
        out_specs=[
            pl.BlockSpec((None, None, tr, N_HEADS, 2, HEAD_DIM), lambda b, i: (layer, b, i, 0, 0, 0)),
            pl.BlockSpec((None, None, tr, N_HEADS, V_DIM), lambda b, i: (layer, b, i, 0, 0)),
        ],
        out_shape=[
            jax.ShapeDtypeStruct((depth, batch, seq, N_HEADS, 2, HEAD_DIM), F32),
            jax.ShapeDtypeStruct((depth, batch, seq, N_HEADS, V_DIM), F32),
        ],
        input_output_aliases=aliases,
        compiler_params=_params(("arbitrary", "arbitrary")),
        name="kv_export",
    )(*args)


def _merge_kernel(a_ref, o_ref, wa_ref, wo_ref, ga_ref, go_ref, out_ref):
    ya = jnp.dot(a_ref[...], wa_ref[...], preferred_element_type=F32)
    yo = jnp.dot(o_ref[...], wo_ref[...], preferred_element_type=F32)
    merged = ga_ref[...].astype(F32) * ya + go_ref[...].astype(F32) * yo
    out_ref[...] = merged.astype(BF16)


def gated_merge(a_out, o_out, w_ssm_all, w_attn_all, gates, layer, tm, tn):
    m, d_br = a_out.shape
    d_model = w_ssm_all.shape[2]
    nj = d_model // tn
    return pl.pallas_call(
        _merge_kernel,
        grid=(m // tm, nj),
        in_specs=[
            pl.BlockSpec((tm, d_br), lambda i, j: (i, 0)),
            pl.BlockSpec((tm, d_br), lambda i, j: (i, 0)),
            pl.BlockSpec((None, d_br, tn), lambda i, j: (layer, 0, j)),
            pl.BlockSpec((None, d_br, tn), lambda i, j: (layer, 0, j)),
            pl.BlockSpec((tm, tn), lambda i, j: (i, j)),
            pl.BlockSpec((tm, tn), lambda i, j: (i, nj + j)),
        ],
        out_specs=pl.BlockSpec((tm, tn), lambda i, j: (i, j)),
        out_shape=jax.ShapeDtypeStruct((m, d_model), BF16),
        compiler_params=_params(("parallel", "arbitrary")),
        name="gated_merge",
    )(a_out, o_out, w_ssm_all, w_attn_all, gates, gates)


def _out_proj_kernel(a_ref, w_ref, x_ref, g_ref, o_ref, *, tn):
    n = o_ref.shape[1]
    a = a_ref[...]
    ssq = jnp.zeros((a.shape[0], 1), F32)
    for c in range(n // tn):
        y = jnp.dot(a, w_ref[:, c * tn:(c + 1) * tn], preferred_element_type=F32)
        ssq = ssq + jnp.sum(y * y, axis=-1, keepdims=True)
        o_ref[:, c * tn:(c + 1) * tn] = y
    inv = lax.rsqrt(ssq / n + EPS)
    o_ref[...] = x_ref[...] + o_ref[...] * inv * g_ref[...]


def out_proj_residual(merged, w_all, x, g_row, layer, tm, tn):
    m, d = x.shape
    k = merged.shape[1]
    return pl.pallas_call(
        functools.partial(_out_proj_kernel, tn=tn),
        grid=(m // tm,),
        in_specs=[
            pl.BlockSpec((tm, k), lambda i: (i, 0)),
            pl.BlockSpec((None, k, d), lambda i: (layer, 0, 0)),
            pl.BlockSpec((tm, d), lambda i: (i, 0)),
            pl.BlockSpec((1, d), lambda i: (0, 0)),
        ],
        out_specs=pl.BlockSpec((tm, d), lambda i: (i, 0)),
        out_shape=jax.ShapeDtypeStruct((m, d), F32),
        compiler_params=_params(("parallel",)),
        name="out_proj_residual",
    )(merged, w_all, x, g_row)


def _ffn_kernel(x_ref, gpre_ref, wg_ref, wu_ref, wd_ref, gpost_ref, o_ref, h_scr):
    j = pl.program_id(1)

    @pl.when(j == 0)
    def _():
        h_scr[...] = _rms(x_ref[...], gpre_ref[...]).astype(BF16)
        o_ref[...] = jnp.zeros(o_ref.shape, F32)

    h = h_scr[...]
    gate = jnp.dot(h, wg_ref[...], preferred_element_type=F32)
    up = jnp.dot(h, wu_ref[...], preferred_element_type=F32)
    act = (gate * _sigmoid(gate) * up).astype(BF16)
    o_ref[...] += jnp.dot(act, wd_ref[...], preferred_element_type=F32)

    @pl.when(j == pl.num_programs(1) - 1)
    def _():
        o_ref[...] = x_ref[...] + _rms(o_ref[...], gpost_ref[...])


def ffn_residual(x, gpre_row, wg_all, wu_all, wd_all, gpost_row, layer, tm, tf):
    m, d = x.shape
    f = wg_all.shape[2]
    return pl.pallas_call(
        _ffn_kernel,
        grid=(m // tm, f // tf),
        in_specs=[
            pl.BlockSpec((tm, d), lambda i, j: (i, 0)),
            pl.BlockSpec((1, d), lambda i, j: (0, 0)),
            pl.BlockSpec((None, d, tf), lambda i, j: (layer, 0, j)),
            pl.BlockSpec((None, d, tf), lambda i, j: (layer, 0, j)),
            pl.BlockSpec((None, tf, d), lambda i, j: (layer, j, 0)),
            pl.BlockSpec((1, d), lambda i, j: (0, 0)),
        ],
        out_specs=pl.BlockSpec((tm, d), lambda i, j: (i, 0)),
        out_shape=jax.ShapeDtypeStruct((m, d), F32),
        scratch_shapes=[pltpu.VMEM((tm, d), BF16)],
        compiler_params=_params(("parallel", "arbitrary")),
        name="swiglu_residual",
    )(x, gpre_row, wg_all, wu_all, wd_all, gpost_row)


def _tile(n, want):
    t = min(n, want)
    assert n % t == 0, (n, want)
    return t


def kernel(x_prompt, x_sample, cache_k, cache_v, state_ssm_re, state_ssm_im, g_pre_mix, w_in, b_gate, ssm_a_re, ssm_a_im, ssm_log_dt, ssm_b_re, ssm_b_im, ssm_c_re, ssm_c_im, ssm_d, w_glu, b_glu, lam_q1, lam_k1, lam_q2, lam_k2, g_sub, w_br_ssm, w_br_attn, w_out, g_post_mix, g_pre_ffn, w_ffn_gate, w_ffn_up, w_ffn_down, g_post_ffn):
    batch, seq, d_model = x_prompt.shape
    dec_batch, dec_seq, _ = x_sample.shape
    depth = w_in.shape[0]
    past = cache_k.shape[2]
    n_groups, state_dim = ssm_a_re.shape[1:]
    n_state = n_groups * state_dim
    assert ssm_d.shape[1] == D_SSM and w_in.shape[2] == COL_GATE * D_QK + N_BRANCH * d_model
    assert w_in.shape[2] % D_QK == 0
    rows_p = seq * batch
    rows_s = dec_seq * dec_batch

    x = jnp.concatenate([x_prompt.reshape(rows_p, d_model),
                         x_sample.reshape(rows_s, d_model)], axis=0)

    w_in_b = w_in.astype(BF16)
    w_glu_b = w_glu.astype(BF16)
    w_br_ssm_b = w_br_ssm.astype(BF16)
    w_br_attn_b = w_br_attn.astype(BF16)
    w_out_b = w_out.astype(BF16)
    w_gate_b = w_ffn_gate.astype(BF16)
    w_up_b = w_ffn_up.astype(BF16)
    w_down_b = w_ffn_down.astype(BF16)

    tm = _tile(rows_s, 1024)
    assert rows_p % tm == 0
    ssm_rows = 256
    zeros_state = jnp.zeros((batch, n_state), F32)

    hr_p, hi_p, hr_s, hi_s = [], [], [], []
    k_p = v_p = k_s = v_s = None
    for l in range(depth):
        lam_init = _lambda_init(l)
        proj, gates = norm_matmul(x, g_pre_mix[l][None], w_in_b,
                                  b_gate[l].reshape(1, N_BRANCH * d_model), l, tm,
                                  _tile(d_model, 512))

        ab_re, ab_im, bc, cc = ssm_prep(ssm_a_re[l], ssm_a_im[l], ssm_log_dt[l], ssm_b_re[l],
                                        ssm_b_im[l], ssm_c_re[l], ssm_c_im[l])
        d_row = ssm_d[l][None]
        bg_row = b_glu[l][None]
        a_p, hrp, hip = ssm_branch(proj, 0, seq, batch, ssm_rows // batch, zeros_state, zeros_state,
                                   ab_re, ab_im, bc, cc, d_row, w_glu_b, l, bg_row)
        a_s, hrs, his = ssm_branch(proj, rows_p, dec_seq, dec_batch, ssm_rows // dec_batch,
                                   state_ssm_re[l].reshape(dec_batch, n_state),
                                   state_ssm_im[l].reshape(dec_batch, n_state),
                                   ab_re, ab_im, bc, cc, d_row, w_glu_b, l, bg_row)
        a_out = jnp.concatenate([a_p, a_s], axis=0)

        lam_rows = (lam_q1[l][None], lam_k1[l][None], lam_q2[l][None], lam_k2[l][None])
        gsub_row = g_sub[l][None]
        o_out = attn_prompt(proj, seq, batch, lam_rows, gsub_row, lam_init, _tile(seq, 512))
        o_out = attn_sample(proj, o_out, rows_p, dec_seq, dec_batch, cache_k, cache_v, l, lam_rows,
                            gsub_row, lam_init, _tile(past, 2048))
        k_p, v_p = kv_export(proj, 0, seq, batch, depth, l, k_p, v_p, _tile(seq, 512))
        k_s, v_s = kv_export(proj, rows_p, dec_seq, dec_batch, depth, l, k_s, v_s, dec_seq)

        merged = gated_merge(a_out, o_out, w_br_ssm_b, w_br_attn_b, gates, l, tm,
                             _tile(d_model, 512))
        x = out_proj_residual(merged, w_out_b, x, g_post_mix[l][None], l, _tile(rows_s, 512),
                              _tile(d_model, 512))
        x = ffn_residual(x, g_pre_ffn[l][None], w_gate_b, w_up_b, w_down_b, g_post_ffn[l][None],
                         l, _tile(rows_s, 512), _tile(w_ffn_gate.shape[2], 512))

        hr_p.append(hrp.reshape(batch, n_groups, state_dim))
        hi_p.append(hip.reshape(batch, n_groups, state_dim))
        hr_s.append(hrs.reshape(dec_batch, n_groups, state_dim))
        hi_s.append(his.reshape(dec_batch, n_groups, state_dim))

    y_p = x[:rows_p].reshape(batch, seq, d_model)
    y_s = x[rows_p:].reshape(dec_batch, dec_seq, d_model)
    return (y_p, y_s, k_p, v_p, jnp.stack(hr_p), jnp.stack(hi_p),
            k_s, v_s, jnp.stack(hr_s), jnp.stack(hi_s))
```

```python
import functools
import math

import jax
import jax.numpy as jnp
from jax import lax
from jax.experimental import pallas as pl
from jax.experimental.pallas import tpu as pltpu

F32 = jnp.float32
BF16 = jnp.bfloat16

CHUNK = 64
SSM_GROUP = 16
STATE_DIM = 64
N_HEADS = 4
HEAD_DIM = 128
V_DIM = 2 * HEAD_DIM
N_BRANCH = 2
EPS = 1e-6
D_SSM = 1024
D_QK = N_HEADS * 2 * HEAD_DIM
D_ATTN = N_HEADS * V_DIM
assert D_SSM == D_QK == D_ATTN
COL_Q, COL_K, COL_V, COL_GATE = 1, 2, 3, 4
CACHE_ROWS = D_QK // HEAD_DIM

V7X_MXU_DIM = 256
V7X_SUBLANES = 8
LANES = 128
V7X_VMEM_LIMIT = 56 * 1024 * 1024

SSM_KTILE = V7X_MXU_DIM
SSM_GROUPS_PER_TILE = SSM_KTILE // SSM_GROUP
SSM_STATE_PER_TILE = SSM_GROUPS_PER_TILE * STATE_DIM


def _params(sem, vmem=V7X_VMEM_LIMIT):
    return pltpu.CompilerParams(dimension_semantics=sem, vmem_limit_bytes=vmem)


def _rms(xf, g):
    ms = jnp.mean(xf * xf, axis=-1, keepdims=True)
    return xf * lax.rsqrt(ms + EPS) * g


def _sigmoid(x):
    return 1.0 / (1.0 + jnp.exp(-x))


def _gelu_tanh(x):
    c = math.sqrt(2.0 / math.pi)
    return x * (0.5 * (1.0 + jnp.tanh(c * (x + 0.044715 * (x * x * x)))))


def _lambda_init(layer):
    return 0.8 - 0.6 * math.exp(-0.3 * layer)


def _alibi_slope(h):
    return 2.0 ** (-8.0 * (h + 1) / N_HEADS)


def _norm_matmul_kernel(x_ref, g_ref, w_ref, bgate_ref, o_ref, gate_ref, h_scr, *, n_main):
    j = pl.program_id(1)

    @pl.when(j == 0)
    def _():
        h_scr[...] = _rms(x_ref[...], g_ref[...]).astype(BF16)

    @pl.when(j < n_main)
    def _():
        o_ref[...] = jnp.dot(h_scr[...], w_ref[...], preferred_element_type=F32)

    @pl.when(j >= n_main)
    def _():
        cw = V7X_MXU_DIM
        for c in range(w_ref.shape[1] // cw):
            cols = slice(c * cw, (c + 1) * cw)
            logits = jnp.dot(h_scr[...], w_ref[:, cols], preferred_element_type=F32)
            gate_ref[:, cols] = _sigmoid(logits + bgate_ref[:, cols]).astype(BF16)


def norm_matmul(x, g, w_all, b_gate_row, layer, tm, tn):
    m, d = x.shape
    n = w_all.shape[2]
    n_main = COL_GATE * D_QK // tn
    n_gate = b_gate_row.shape[1] // tn
    assert n_main + n_gate == n // tn
    return pl.pallas_call(
        functools.partial(_norm_matmul_kernel, n_main=n_main),
        grid=(m // tm, n // tn),
        in_specs=[
            pl.BlockSpec((tm, d), lambda i, j: (i, 0)),
            pl.BlockSpec((1, d), lambda i, j: (0, 0)),
            pl.BlockSpec((None, d, tn), lambda i, j: (layer, 0, j)),
            pl.BlockSpec((1, tn), lambda i, j: (0, jnp.maximum(j - n_main, 0))),
        ],
        out_specs=[
            pl.BlockSpec((tm, tn), lambda i, j: (i, jnp.minimum(j, n_main - 1))),
            pl.BlockSpec((tm, tn), lambda i, j: (i, jnp.maximum(j - n_main, 0))),
        ],
        out_shape=[jax.ShapeDtypeStruct((m, n_main * tn), F32),
                   jax.ShapeDtypeStruct((m, n_gate * tn), BF16)],
        scratch_shapes=[pltpu.VMEM((tm, d), BF16)],
        compiler_params=_params(("arbitrary", "arbitrary")),
        name="norm_in_proj",
    )(x, g, w_all, b_gate_row)


def _ssm_prep_kernel(are_ref, aim_ref, ldt_ref, bre_ref, bim_ref, cre_ref, cim_ref,
                     abre_ref, abim_ref, bc_ref, cc_ref):
    a_re = are_ref[...]
    a_im = aim_ref[...]
    dt = jnp.exp(ldt_ref[...])
    mag = jnp.exp(dt * a_re)
    ab_re = mag * jnp.cos(dt * a_im)
    ab_im = mag * jnp.sin(dt * a_im)
    n_re = ab_re - 1.0
    n_im = ab_im
    den = a_re * a_re + a_im * a_im
    c_re = (n_re * a_re + n_im * a_im) / den
    c_im = (n_im * a_re - n_re * a_im) / den
    abre_ref[...] = ab_re
    abim_ref[...] = ab_im
    b_re = bre_ref[...]
    b_im = bim_ref[...]
    ns = b_re.shape[1]
    bc_ref[:, :ns] = (c_re * b_re - c_im * b_im).astype(BF16)
    bc_ref[:, ns:] = (c_re * b_im + c_im * b_re).astype(BF16)
    cc_ref[:ns, :] = cre_ref[...].astype(BF16)
    cc_ref[ns:, :] = (-cim_ref[...]).astype(BF16)


def ssm_prep(a_re, a_im, log_dt, b_re, b_im, c_re, c_im):
    g, p = a_re.shape
    kt = g // SSM_GROUPS_PER_TILE
    ns = SSM_STATE_PER_TILE
    eye = jnp.eye(SSM_GROUPS_PER_TILE, dtype=F32)

    def place_b(b):
        bt = b.reshape(kt, SSM_GROUPS_PER_TILE, p, SSM_GROUP).transpose(0, 1, 3, 2)
        return jnp.einsum('kgcp,gh->kgchp', bt, eye).reshape(kt, SSM_KTILE, ns)

    def place_c(c):
        ct = c.reshape(kt, SSM_GROUPS_PER_TILE, SSM_GROUP, p).transpose(0, 1, 3, 2)
        return jnp.einsum('kgpc,gh->kgphc', ct, eye).reshape(kt, ns, SSM_KTILE)

    row = lambda a: a.reshape(1, g * p)
    ldt = row(jnp.broadcast_to(log_dt[:, None], (g, p)))
    vec = pl.BlockSpec((1, ns), lambda k: (0, k))
    bspec = pl.BlockSpec((None, SSM_KTILE, ns), lambda k: (k, 0, 0))
    cspec = pl.BlockSpec((None, ns, SSM_KTILE), lambda k: (k, 0, 0))
    return pl.pallas_call(
        _ssm_prep_kernel,
        grid=(kt,),
        in_specs=[vec, vec, vec, bspec, bspec, cspec, cspec],
        out_specs=[vec, vec,
                   pl.BlockSpec((None, SSM_KTILE, 2 * ns), lambda k: (k, 0, 0)),
                   pl.BlockSpec((None, 2 * ns, SSM_KTILE), lambda k: (k, 0, 0))],
        out_shape=[jax.ShapeDtypeStruct((1, g * p), F32),
                   jax.ShapeDtypeStruct((1, g * p), F32),
                   jax.ShapeDtypeStruct((kt, SSM_KTILE, 2 * ns), BF16),
                   jax.ShapeDtypeStruct((kt, 2 * ns, SSM_KTILE), BF16)],
        compiler_params=_params(("arbitrary",)),
        name="ssm_discretise",
    )(row(a_re), row(a_im), ldt, place_b(b_re), place_b(b_im), place_c(c_re), place_c(c_im))


SSM_SCAN_LANES = 256


def _ssm_kernel(*refs, nb, n_kt):
    u_refs = refs[:nb]
    (h0r_ref, h0i_ref, ar_ref, ai_ref, bc_ref, cc_ref, d_ref, wg_ref, bg_ref,
     out_ref, hr_ref, hi_ref, io_scr, hs_scr) = refs[nb:]
    ns = SSM_STATE_PER_TILE
    tc, d_ssm = u_refs[0].shape
    rows = nb * tc
    n_lt = d_ssm // LANES

    @pl.when(pl.program_id(0) == 0)
    def _():
        hr_ref[...] = h0r_ref[...]
        hi_ref[...] = h0i_ref[...]

    for b in range(nb):
        for j in range(n_lt):
            io_scr[j, pl.ds(b, tc, stride=nb), :] = u_refs[b][:, j * LANES:(j + 1) * LANES]
    u = jnp.concatenate([io_scr[j] for j in range(n_lt)], axis=1)
    ub = u.astype(BF16)
    def input_map(kt):
        hs_scr[:, kt * 2 * ns:(kt + 1) * 2 * ns] = jnp.dot(
            ub[:, kt * SSM_KTILE:(kt + 1) * SSM_KTILE], bc_ref[kt], preferred_element_type=F32)

    rpi = hr_ref.shape[0]
    two_step = rpi != nb
    n_iter = rows // rpi
    w = SSM_SCAN_LANES
    low = lax.broadcasted_iota(jnp.int32, (rpi, w), 0) < nb

    def recurrence(kt):
        for s in range(ns // w):
            re0 = kt * 2 * ns + s * w
            im0 = re0 + ns
            a0 = kt * ns + s * w
            ar = jnp.broadcast_to(ar_ref[:, a0:a0 + w], (rpi, w))
            ai = jnp.broadcast_to(ai_ref[:, a0:a0 + w], (rpi, w))
            hr = hr_ref[:, a0:a0 + w]
            hi = hi_ref[:, a0:a0 + w]
            for it in range(n_iter):
                r0 = it * rpi
                br = hs_scr[r0:r0 + rpi, re0:re0 + w]
                bi = hs_scr[r0:r0 + rpi, im0:im0 + w]
                nr = ar * hr - ai * hi + br
                ni = ar * hi + ai * hr + bi
                if two_step:
                    sr = pltpu.roll(nr, nb, 0)
                    si = pltpu.roll(ni, nb, 0)
                    mr = ar * sr - ai * si + br
                    mi = ar * si + ai * sr + bi
                    hs_scr[r0:r0 + rpi, re0:re0 + w] = jnp.where(low, nr, mr)
                    hs_scr[r0:r0 + rpi, im0:im0 + w] = jnp.where(low, ni, mi)
                    hr = pltpu.roll(mr, nb, 0)
                    hi = pltpu.roll(mi, nb, 0)
                else:
                    hs_scr[r0:r0 + rpi, re0:re0 + w] = nr
                    hs_scr[r0:r0 + rpi, im0:im0 + w] = ni
                    hr, hi = nr, ni
            hr_ref[:, a0:a0 + w] = hr
            hi_ref[:, a0:a0 + w] = hi

    cols = []
    input_map(0)
    for kt in range(n_kt):
        if kt + 1 < n_kt:
            input_map(kt + 1)
        recurrence(kt)
        hk = hs_scr[:, kt * 2 * ns:(kt + 1) * 2 * ns].astype(BF16)
        cols.append(jnp.dot(hk, cc_ref[kt], preferred_element_type=F32))
    y = jnp.concatenate(cols, axis=1) + d_ref[...] * u
    z = _gelu_tanh(y)
    gate = _sigmoid(jnp.dot(z.astype(BF16), wg_ref[...], preferred_element_type=F32) + bg_ref[...])
    out = z * gate
    for j in range(n_lt):
        io_scr[j] = out[:, j * LANES:(j + 1) * LANES]
    for b in range(nb):
        out_ref[b] = jnp.concatenate(
            [io_scr[j, pl.ds(b, tc, stride=nb), :] for j in range(n_lt)], axis=1).astype(BF16)


def ssm_branch(proj, row0, seq, nb, tc, h0_re, h0_im, ab_re, ab_im, bc, cc, d_row, w_glu_all,
               layer, b_glu_row):
    d_ssm = d_row.shape[1]
    n_kt = d_ssm // SSM_KTILE
    n_state = ab_re.shape[1]
    full = lambda a: pl.BlockSpec(a.shape, lambda c: (0,) * a.ndim)
    srows = max(nb, V7X_SUBLANES)
    assert srows == nb or srows == 2 * nb
    if srows != nb:
        h0_re = jnp.concatenate([h0_re, h0_re], axis=0)
        h0_im = jnp.concatenate([h0_im, h0_im], axis=0)
    n_chunks = seq // tc
    rb0 = row0 // tc
    u_specs = [pl.BlockSpec((tc, d_ssm), lambda c, b=b: (rb0 + b * n_chunks + c, 0))
               for b in range(nb)]
    a_out, h_re, h_im = pl.pallas_call(
        functools.partial(_ssm_kernel, nb=nb, n_kt=n_kt),
        grid=(n_chunks,),
        in_specs=u_specs + [
            full(h0_re), full(h0_im), full(ab_re), full(ab_im), full(bc), full(cc), full(d_row),
            pl.BlockSpec((None, d_ssm, d_ssm), lambda c: (layer, 0, 0)),
            full(b_glu_row),
        ],
        out_specs=[
            pl.BlockSpec((nb, tc, d_ssm), lambda c: (0, c, 0)),
            pl.BlockSpec((srows, n_state), lambda c: (0, 0)),
            pl.BlockSpec((srows, n_state), lambda c: (0, 0)),
        ],
        out_shape=[
            jax.ShapeDtypeStruct((nb, seq, d_ssm), BF16),
            jax.ShapeDtypeStruct((srows, n_state), F32),
            jax.ShapeDtypeStruct((srows, n_state), F32),
        ],
        scratch_shapes=[pltpu.VMEM((d_ssm // LANES, nb * tc, LANES), F32),
                        pltpu.VMEM((nb * tc, 2 * n_state), F32)],
        compiler_params=_params(("arbitrary",)),
        name="ssm_scan",
    )(*([proj] * nb), h0_re, h0_im, ab_re, ab_im, bc, cc, d_row, w_glu_all, b_glu_row)
    return a_out.reshape(nb * seq, d_ssm), h_re[:nb], h_im[:nb]


def _diff_lambda(lq1_ref, lk1_ref, lq2_ref, lk2_ref, lam_init):
    s1 = jnp.sum(lq1_ref[...] * lk1_ref[...], axis=-1, keepdims=True)
    s2 = jnp.sum(lq2_ref[...] * lk2_ref[...], axis=-1, keepdims=True)
    return jnp.exp(s1) - jnp.exp(s2) + lam_init


def _masked_distance(qpos, kpos):
    dist = jnp.abs(qpos - kpos).astype(F32)
    shift = CHUNK.bit_length() - 1
    assert CHUNK == 1 << shift
    allowed = jnp.right_shift(kpos, shift) <= jnp.right_shift(qpos, shift)
    return jnp.where(allowed, dist, jnp.inf)


def _softmax_step(hm, s, m_scr, l_scr):
    m_old = m_scr[hm]
    m_new = jnp.maximum(m_old, jnp.max(s, axis=-1, keepdims=True))
    alpha = jnp.exp(m_old - m_new)
    p = jnp.exp(s - m_new)
    l_scr[hm] = alpha * l_scr[hm] + jnp.sum(p, axis=-1, keepdims=True)
    m_scr[hm] = m_new
    return p.astype(BF16), alpha


def _attn_tile(q_scr, k_tile, v_tile, distm, m_scr, l_scr, acc_scr):
    scale = HEAD_DIM ** -0.5
    rows = q_scr.shape[0]
    for h in range(N_HEADS):
        bias = (-_alibi_slope(h)) * distm
        ps, alphas = [], []
        for mp in range(2):
            hm = 2 * h + mp
            q = q_scr[:, hm * HEAD_DIM:(hm + 1) * HEAD_DIM]
            s = lax.dot_general(q, k_tile(hm), (((1,), (1,)), ((), ())),
                                preferred_element_type=F32) * scale + bias
            p, alpha = _softmax_step(hm, s, m_scr, l_scr)
            ps.append(p)
            alphas.append(alpha)
        pv = jnp.dot(jnp.concatenate(ps, axis=0), v_tile(h), preferred_element_type=F32)
        for mp in range(2):
            hm = 2 * h + mp
            acc_scr[hm] = alphas[mp] * acc_scr[hm] + pv[mp * rows:(mp + 1) * rows]


def _attn_init(m_scr, l_scr, acc_scr):
    m_scr[...] = jnp.full(m_scr.shape, -jnp.inf, F32)
    l_scr[...] = jnp.zeros(l_scr.shape, F32)
    acc_scr[...] = jnp.zeros(acc_scr.shape, F32)


def _attn_finish(o_ref, lam, gsub_ref, lam_init, m_scr, l_scr, acc_scr):
    for h in range(N_HEADS):
        o = acc_scr[2 * h] / l_scr[2 * h] - lam * (acc_scr[2 * h + 1] / l_scr[2 * h + 1])
        o = _rms(o, gsub_ref[...]) * (1.0 - lam_init)
        o_ref[:, h * V_DIM:(h + 1) * V_DIM] = o.astype(BF16)


def _fold_lanes(x, op):
    out = x[:, :LANES]
    for c in range(1, x.shape[1] // LANES):
        out = op(out, x[:, c * LANES:(c + 1) * LANES])
    return out


def _attn_prompt_kernel(q_ref, k_ref, v_ref, lq1_ref, lk1_ref, lq2_ref, lk2_ref, gsub_ref,
                        o_ref, q_scr, k_scr, v_scr, s_scr, m_scr, l_scr, acc_scr, *, lam_init):
    i = pl.program_id(1)
    tq = q_ref.shape[0]
    scale = HEAD_DIM ** -0.5

    @pl.when(i == 0)
    def _():
        k_scr[...] = k_ref[...].astype(BF16)
        v_scr[...] = v_ref[...].astype(BF16)

    q_scr[...] = q_ref[...].astype(BF16)
    row = lax.broadcasted_iota(jnp.int32, (tq, tq), 0)
    col = lax.broadcasted_iota(jnp.int32, (tq, tq), 1)
    rel = (row - col).astype(F32)
    dist_diag = _masked_distance(i * tq + row, i * tq + col)
    lam = _diff_lambda(lq1_ref, lk1_ref, lq2_ref, lk2_ref, lam_init)

    def scores(h, mp, k0, bias):
        hm = 2 * h + mp
        q = q_scr[:, hm * HEAD_DIM:(hm + 1) * HEAD_DIM]
        k = k_scr[pl.ds(k0, tq), hm * HEAD_DIM:(hm + 1) * HEAD_DIM]
        return lax.dot_general(q, k, (((1,), (1,)), ((), ())),
                               preferred_element_type=F32) * scale + bias

    for h in range(N_HEADS):
        neg_slope = -_alibi_slope(h)
        d0 = pl.multiple_of(i * tq, tq)
        bias_diag = neg_slope * dist_diag
        for mp in range(2):
            s = scores(h, mp, d0, bias_diag)
            s_scr[mp, :, pl.ds(d0, tq)] = s
            m_scr[mp] = _fold_lanes(s, jnp.maximum)
        base = neg_slope * rel

        def pass1(j, carry, h=h, base=base, neg_slope=neg_slope):
            k0 = pl.multiple_of(j * tq, tq)
            bias = base + neg_slope * ((i - j) * tq).astype(F32)
            for mp in range(2):
                s = scores(h, mp, k0, bias)
                s_scr[mp, :, pl.ds(k0, tq)] = s
                m_scr[mp] = jnp.maximum(m_scr[mp], _fold_lanes(s, jnp.maximum))
            return carry

        lax.fori_loop(0, i, pass1, 0)
        m_row = [jnp.max(m_scr[mp], axis=-1, keepdims=True) for mp in range(2)]
        l_scr[...] = jnp.zeros(l_scr.shape, F32)
        acc_scr[...] = jnp.zeros(acc_scr.shape, F32)

        def pass2(j, carry, h=h, m_row=m_row):
            k0 = pl.multiple_of(j * tq, tq)
            v = v_scr[pl.ds(k0, tq), h * V_DIM:(h + 1) * V_DIM]
            for mp in range(2):
                p = jnp.exp(s_scr[mp, :, pl.ds(k0, tq)] - m_row[mp])
                l_scr[mp] += _fold_lanes(p, jnp.add)
                acc_scr[mp] += jnp.dot(p.astype(BF16), v, preferred_element_type=F32)
            return carry

        lax.fori_loop(0, i + 1, pass2, 0)
        l_row = [jnp.sum(l_scr[mp], axis=-1, keepdims=True) for mp in range(2)]
        o = acc_scr[0] / l_row[0] - lam * (acc_scr[1] / l_row[1])
        o = _rms(o, gsub_ref[...]) * (1.0 - lam_init)
        o_ref[:, h * V_DIM:(h + 1) * V_DIM] = o.astype(BF16)


def attn_prompt(proj, seq, batch, lam_rows, gsub_row, lam_init, tq):
    rows = proj.shape[0]
    nq = seq // tq
    lam_spec = pl.BlockSpec((1, HEAD_DIM), lambda b, i: (0, 0))
    return pl.pallas_call(
        functools.partial(_attn_prompt_kernel, lam_init=lam_init),
        grid=(batch, nq),
        in_specs=[
            pl.BlockSpec((tq, D_QK), lambda b, i: (b * nq + i, COL_Q)),
            pl.BlockSpec((seq, D_QK), lambda b, i: (b, COL_K), pipeline_mode=pl.Buffered(1)),
            pl.BlockSpec((seq, D_ATTN), lambda b, i: (b, COL_V), pipeline_mode=pl.Buffered(1)),
            lam_spec, lam_spec, lam_spec, lam_spec,
            pl.BlockSpec((1, V_DIM), lambda b, i: (0, 0)),
        ],
        out_specs=pl.BlockSpec((tq, D_ATTN), lambda b, i: (b * nq + i, 0)),
        out_shape=jax.ShapeDtypeStruct((rows, D_ATTN), BF16),
        scratch_shapes=[
            pltpu.VMEM((tq, D_QK), BF16),
            pltpu.VMEM((seq, D_QK), BF16),
            pltpu.VMEM((seq, D_ATTN), BF16),
            pltpu.VMEM((2, tq, seq), F32),
            pltpu.VMEM((2, tq, 128), F32),
            pltpu.VMEM((2, tq, 128), F32),
            pltpu.VMEM((2, tq, V_DIM), F32),
        ],
        compiler_params=_params(("arbitrary", "arbitrary")),
        name="attn_prompt",
    )(proj, proj, proj, *lam_rows, gsub_row)


def _attn_sample_kernel(q_ref, kn_ref, vn_ref, kp_ref, vp_ref, lq1_ref, lk1_ref, lq2_ref, lk2_ref,
                        gsub_ref, o_prev_ref, o_ref, q_scr, m_scr, l_scr, acc_scr, *, lam_init, past):
    del o_prev_ref
    j = pl.program_id(1)
    s_len = q_ref.shape[0]
    tk = kp_ref.shape[0] // CACHE_ROWS

    @pl.when(j == 0)
    def _():
        q_scr[...] = q_ref[...].astype(BF16)
        _attn_init(m_scr, l_scr, acc_scr)

    def v_past(h):
        halves = [vp_ref[pl.ds(c * N_HEADS + h, tk, stride=CACHE_ROWS), :] for c in range(2)]
        return jnp.concatenate(halves, axis=1).astype(BF16)

    qpos = past + lax.broadcasted_iota(jnp.int32, (s_len, tk), 0)
    kpos = j * tk + lax.broadcasted_iota(jnp.int32, (s_len, tk), 1)
    _attn_tile(
        q_scr,
        lambda hm: kp_ref[pl.ds(hm, tk, stride=CACHE_ROWS), :].astype(BF16),
        v_past,
        _masked_distance(qpos, kpos), m_scr, l_scr, acc_scr)

    @pl.when(j == pl.num_programs(1) - 1)
    def _():
        qn = past + lax.broadcasted_iota(jnp.int32, (s_len, s_len), 0)
        kn = past + lax.broadcasted_iota(jnp.int32, (s_len, s_len), 1)
        _attn_tile(
            q_scr,
            lambda hm: kn_ref[:, hm * HEAD_DIM:(hm + 1) * HEAD_DIM].astype(BF16),
            lambda h: vn_ref[:, h * V_DIM:(h + 1) * V_DIM].astype(BF16),
            _masked_distance(qn, kn), m_scr, l_scr, acc_scr)
        lam = _diff_lambda(lq1_ref, lk1_ref, lq2_ref, lk2_ref, lam_init)
        _attn_finish(o_ref, lam, gsub_ref, lam_init, m_scr, l_scr, acc_scr)


def attn_sample(proj, o_prev, row0, s_len, batch, cache_k, cache_v, layer, lam_rows, gsub_row,
                lam_init, tk):
    rows = proj.shape[0]
    depth, _, past = cache_k.shape[:3]
    rb = row0 // s_len
    lam_spec = pl.BlockSpec((1, HEAD_DIM), lambda b, j: (0, 0))
    k_rows = cache_k.reshape(depth, batch, past * CACHE_ROWS, HEAD_DIM)
    v_rows = (cache_v.reshape(depth, batch, past, N_HEADS, 2, HEAD_DIM)
              .transpose(0, 1, 2, 4, 3, 5).reshape(depth, batch, past * CACHE_ROWS, HEAD_DIM))
    cache_spec = pl.BlockSpec((None, None, tk * CACHE_ROWS, HEAD_DIM), lambda b, j: (layer, b, j, 0))
    return pl.pallas_call(
        functools.partial(_attn_sample_kernel, lam_init=lam_init, past=past),
        grid=(batch, past // tk),
        in_specs=[
            pl.BlockSpec((s_len, D_QK), lambda b, j: (rb + b, COL_Q)),
            pl.BlockSpec((s_len, D_QK), lambda b, j: (rb + b, COL_K)),
            pl.BlockSpec((s_len, D_ATTN), lambda b, j: (rb + b, COL_V)),
            cache_spec, cache_spec,
            lam_spec, lam_spec, lam_spec, lam_spec,
            pl.BlockSpec((1, V_DIM), lambda b, j: (0, 0)),
            pl.BlockSpec(memory_space=pl.ANY),
        ],
        out_specs=pl.BlockSpec((s_len, D_ATTN), lambda b, j: (rb + b, 0)),
        out_shape=jax.ShapeDtypeStruct((rows, D_ATTN), BF16),
        input_output_aliases={10: 0},
        scratch_shapes=[
            pltpu.VMEM((s_len, D_QK), BF16),
            pltpu.VMEM((2 * N_HEADS, s_len, 1), F32),
            pltpu.VMEM((2 * N_HEADS, s_len, 1), F32),
            pltpu.VMEM((2 * N_HEADS, s_len, V_DIM), F32),
        ],
        compiler_params=_params(("arbitrary", "arbitrary")),
        name="attn_sample",
    )(proj, proj, proj, k_rows, v_rows, *lam_rows, gsub_row, o_prev)


def _kv_export_kernel(k_ref, v_ref, *refs):
    ko_ref, vo_ref = refs[-2:]
    for h in range(N_HEADS):
        vo_ref[:, h, :] = v_ref[:, h * V_DIM:(h + 1) * V_DIM]
        for mp in range(2):
            hm = 2 * h + mp
            ko_ref[:, h, mp, :] = k_ref[:, hm * HEAD_DIM:(hm + 1) * HEAD_DIM]


def kv_export(proj, row0, seq, batch, depth, layer, k_prev, v_prev, tr):
    rb = row0 // tr
    nr = seq // tr
    in_specs = [
        pl.BlockSpec((tr, D_QK), lambda b, i: (rb + b * nr + i, COL_K)),
        pl.BlockSpec((tr, D_ATTN), lambda b, i: (rb + b * nr + i, COL_V)),
    ]
    args = [proj, proj]
    aliases = {}
    if k_prev is not None:
        in_specs += [pl.BlockSpec(memory_space=pl.ANY), pl.BlockSpec(memory_space=pl.ANY)]
        args += [k_prev, v_prev]
        aliases = {2: 0, 3: 1}
    return pl.pallas_call(
        _kv_export_kernel,
        grid=(batch, seq // tr),
        in_specs=in_specs,
        out_specs=[
            pl.BlockSpec((None, None, tr, N_HEADS, 2, HEAD_DIM), lambda b, i: (layer, b, i, 0, 0, 0)),
            pl.BlockSpec((None, None, tr, N_HEADS, V_DIM), lambda b, i: (layer, b, i, 0, 0)),
        ],
        out_shape=[
            jax.ShapeDtypeStruct((depth, batch, seq, N_HEADS, 2, HEAD_DIM), F32),
            jax.ShapeDtypeStruct((depth, batch, seq, N_HEADS, V_DIM), F32),
        ],
        input_output_aliases=aliases,
        compiler_params=_params(("arbitrary", "arbitrary")),
        name="kv_export",
    )(*args)


def _merge_kernel(a_ref, o_ref, wa_ref, wo_ref, ga_ref, go_ref, out_ref):
    ya = jnp.dot(a_ref[...], wa_ref[...], preferred_element_type=F32)
    yo = jnp.dot(o_ref[...], wo_ref[...], preferred_element_type=F32)
    merged = ga_ref[...].astype(F32) * ya + go_ref[...].astype(F32) * yo
    out_ref[...] = merged.astype(BF16)


def gated_merge(a_out, o_out, w_ssm_all, w_attn_all, gates, layer, tm, tn):
    m, d_br = a_out.shape
    d_model = w_ssm_all.shape[2]
    nj = d_model // tn
    return pl.pallas_call(
        _merge_kernel,
        grid=(m // tm, nj),
        in_specs=[
            pl.BlockSpec((tm, d_br), lambda i, j: (i, 0)),
            pl.BlockSpec((tm, d_br), lambda i, j: (i, 0)),
            pl.BlockSpec((None, d_br, tn), lambda i, j: (layer, 0, j)),
            pl.BlockSpec((None, d_br, tn), lambda i, j: (layer, 0, j)),
            pl.BlockSpec((tm, tn), lambda i, j: (i, j)),
            pl.BlockSpec((tm, tn), lambda i, j: (i, nj + j)),
        ],
        out_specs=pl.BlockSpec((tm, tn), lambda i, j: (i, j)),
        out_shape=jax.ShapeDtypeStruct((m, d_model), BF16),
        compiler_params=_params(("parallel", "arbitrary")),
        name="gated_merge",
    )(a_out, o_out, w_ssm_all, w_attn_all, gates, gates)


def _out_proj_kernel(a_ref, w_ref, x_ref, g_ref, o_ref, *, tn):
    n = o_ref.shape[1]
    a = a_ref[...]
    ssq = jnp.zeros((a.shape[0], 1), F32)
    for c in range(n // tn):
        y = jnp.dot(a, w_ref[:, c * tn:(c + 1) * tn], preferred_element_type=F32)
        ssq = ssq + jnp.sum(y * y, axis=-1, keepdims=True)
        o_ref[:, c * tn:(c + 1) * tn] = y
    inv = lax.rsqrt(ssq / n + EPS)
    o_ref[...] = x_ref[...] + o_ref[...] * inv * g_ref[...]


def out_proj_residual(merged, w_all, x, g_row, layer, tm, tn):
    m, d = x.shape
    k = merged.shape[1]
    return pl.pallas_call(
        functools.partial(_out_proj_kernel, tn=tn),
        grid=(m // tm,),
        in_specs=[
            pl.BlockSpec((tm, k), lambda i: (i, 0)),
            pl.BlockSpec((None, k, d), lambda i: (layer, 0, 0)),
            pl.BlockSpec((tm, d), lambda i: (i, 0)),
            pl.BlockSpec((1, d), lambda i: (0, 0)),
        ],
        out_specs=pl.BlockSpec((tm, d), lambda i: (i, 0)),
        out_shape=jax.ShapeDtypeStruct((m, d), F32),
        compiler_params=_params(("parallel",)),
        name="out_proj_residual",
    )(merged, w_all, x, g_row)


def _ffn_kernel(x_ref, gpre_ref, wg_ref, wu_ref, wd_ref, gpost_ref, o_ref, h_scr):
    j = pl.program_id(1)

    @pl.when(j == 0)
    def _():
        h_scr[...] = _rms(x_ref[...], gpre_ref[...]).astype(BF16)
        o_ref[...] = jnp.zeros(o_ref.shape, F32)

    h = h_scr[...]
    gate = jnp.dot(h, wg_ref[...], preferred_element_type=F32)
    up = jnp.dot(h, wu_ref[...], preferred_element_type=F32)
    act = (gate * _sigmoid(gate) * up).astype(BF16)
    o_ref[...] += jnp.dot(act, wd_ref[...], preferred_element_type=F32)

    @pl.when(j == pl.num_programs(1) - 1)
    def _():
        o_ref[...] = x_ref[...] + _rms(o_ref[...], gpost_ref[...])


def ffn_residual(x, gpre_row, wg_all, wu_all, wd_all, gpost_row, layer, tm, tf):
    m, d = x.shape
    f = wg_all.shape[2]
    return pl.pallas_call(
        _ffn_kernel,
        grid=(m // tm, f // tf),
        in_specs=[
            pl.BlockSpec((tm, d), lambda i, j: (i, 0)),
            pl.BlockSpec((1, d), lambda i, j: (0, 0)),
            pl.BlockSpec((None, d, tf), lambda i, j: (layer, 0, j)),
            pl.BlockSpec((None, d, tf), lambda i, j: (layer, 0, j)),
            pl.BlockSpec((None, tf, d), lambda i, j: (layer, j, 0)),
            pl.BlockSpec((1, d), lambda i, j: (0, 0)),
        ],
        out_specs=pl.BlockSpec((tm, d), lambda i, j: (i, 0)),
        out_shape=jax.ShapeDtypeStruct((m, d), F32),
        scratch_shapes=[pltpu.VMEM((tm, d), BF16)],
        compiler_params=_params(("parallel", "arbitrary")),
        name="swiglu_residual",
    )(x, gpre_row, wg_all, wu_all, wd_all, gpost_row)


def _tile(n, want):
    t = min(n, want)
    assert n % t == 0, (n, want)
    return t


def kernel(x_prompt, x_sample, cache_k, cache_v, state_ssm_re, state_ssm_im, g_pre_mix, w_in, b_gate, ssm_a_re, ssm_a_im, ssm_log_dt, ssm_b_re, ssm_b_im, ssm_c_re, ssm_c_im, ssm_d, w_glu, b_glu, lam_q1, lam_k1, lam_q2, lam_k2, g_sub, w_br_ssm, w_br_attn, w_out, g_post_mix, g_pre_ffn, w_ffn_gate, w_ffn_up, w_ffn_down, g_post_ffn):
    batch, seq, d_model = x_prompt.shape
    dec_batch, dec_seq, _ = x_sample.shape
    depth = w_in.shape[0]
    past = cache_k.shape[2]
    n_groups, state_dim = ssm_a_re.shape[1:]
    n_state = n_groups * state_dim
    assert ssm_d.shape[1] == D_SSM and w_in.shape[2] == COL_GATE * D_QK + N_BRANCH * d_model
    assert w_in.shape[2] % D_QK == 0
    rows_p = seq * batch
    rows_s = dec_seq * dec_batch

    x = jnp.concatenate([x_prompt.reshape(rows_p, d_model),
                         x_sample.reshape(rows_s, d_model)], axis=0)

    w_in_b = w_in.astype(BF16)
    w_glu_b = w_glu.astype(BF16)
    w_br_ssm_b = w_br_ssm.astype(BF16)
    w_br_attn_b = w_br_attn.astype(BF16)
    w_out_b = w_out.astype(BF16)
    w_gate_b = w_ffn_gate.astype(BF16)
    w_up_b = w_ffn_up.astype(BF16)
    w_down_b = w_ffn_down.astype(BF16)

    tm = _tile(rows_s, 1024)
    assert rows_p % tm == 0
    ssm_rows = 256
    zeros_state = jnp.zeros((batch, n_state), F32)

    hr_p, hi_p, hr_s, hi_s = [], [], [], []
    k_p = v_p = k_s = v_s = None
    for l in range(depth):
        lam_init = _lambda_init(l)
        proj, gates = norm_matmul(x, g_pre_mix[l][None], w_in_b,
                                  b_gate[l].reshape(1, N_BRANCH * d_model), l, tm,
                                  _tile(N_BRANCH * d_model, 1024))

        ab_re, ab_im, bc, cc = ssm_prep(ssm_a_re[l], ssm_a_im[l], ssm_log_dt[l], ssm_b_re[l],
                                        ssm_b_im[l], ssm_c_re[l], ssm_c_im[l])
        d_row = ssm_d[l][None]
        bg_row = b_glu[l][None]
        a_p, hrp, hip = ssm_branch(proj, 0, seq, batch, ssm_rows // batch, zeros_state, zeros_state,
                                   ab_re, ab_im, bc, cc, d_row, w_glu_b, l, bg_row)
        a_s, hrs, his = ssm_branch(proj, rows_p, dec_seq, dec_batch, ssm_rows // dec_batch,
                                   state_ssm_re[l].reshape(dec_batch, n_state),
                                   state_ssm_im[l].reshape(dec_batch, n_state),
                                   ab_re, ab_im, bc, cc, d_row, w_glu_b, l, bg_row)
        a_out = jnp.concatenate([a_p, a_s], axis=0)

        lam_rows = (lam_q1[l][None], lam_k1[l][None], lam_q2[l][None], lam_k2[l][None])
        gsub_row = g_sub[l][None]
        o_out = attn_prompt(proj, seq, batch, lam_rows, gsub_row, lam_init, _tile(seq, 512))
        o_out = attn_sample(proj, o_out, rows_p, dec_seq, dec_batch, cache_k, cache_v, l, lam_rows,
                            gsub_row, lam_init, _tile(past, 2048))
        k_p, v_p = kv_export(proj, 0, seq, batch, depth, l, k_p, v_p, _tile(seq, 512))
        k_s, v_s = kv_export(proj, rows_p, dec_seq, dec_batch, depth, l, k_s, v_s, dec_seq)

        merged = gated_merge(a_out, o_out, w_br_ssm_b, w_br_attn_b, gates, l, tm,
                             _tile(d_model, 512))
        x = out_proj_residual(merged, w_out_b, x, g_post_mix[l][None], l, _tile(rows_s, 512),
                              _tile(d_model, 512))
        x = ffn_residual(x, g_pre_ffn[l][None], w_gate_b, w_up_b, w_down_b, g_post_ffn[l][None],
                         l, _tile(rows_s, 512), _tile(w_ffn_gate.shape[2], 512))

        hr_p.append(hrp.reshape(batch, n_groups, state_dim))
        hi_p.append(hip.reshape(batch, n_groups, state_dim))
        hr_s.append(hrs.reshape(dec_batch, n_groups, state_dim))
        hi_s.append(his.reshape(dec_batch, n_groups, state_dim))

    y_p = x[:rows_p].reshape(batch, seq, d_model)
    y_s = x[rows_p:].reshape(dec_batch, dec_seq, d_model)
    return (y_p, y_s, k_p, v_p, jnp.stack(hr_p), jnp.stack(hi_p),
            k_s, v_s, jnp.stack(hr_s), jnp.stack(hi_s))
```

```python
import functools
import math

import jax
import jax.numpy as jnp
from jax import lax
from jax.experimental import pallas as pl
from jax.experimental.pallas import tpu as pltpu

F32 = jnp.float32
BF16 = jnp.bfloat16

CHUNK = 64
SSM_GROUP = 16
STATE_DIM = 64
N_HEADS = 4
HEAD_DIM = 128
V_DIM = 2 * HEAD_DIM
N_BRANCH = 2
EPS = 1e-6
D_SSM = 1024
D_QK = N_HEADS * 2 * HEAD_DIM
D_ATTN = N_HEADS * V_DIM
assert D_SSM == D_QK == D_ATTN
COL_Q, COL_K, COL_V, COL_GATE = 1, 2, 3, 4
CACHE_ROWS = D_QK // HEAD_DIM

V7X_MXU_DIM = 256
V7X_SUBLANES = 8
LANES = 128
V7X_VMEM_LIMIT = 56 * 1024 * 1024

SSM_KTILE = V7X_MXU_DIM
SSM_GROUPS_PER_TILE = SSM_KTILE // SSM_GROUP
SSM_STATE_PER_TILE = SSM_GROUPS_PER_TILE * STATE_DIM


def _params(sem, vmem=V7X_VMEM_LIMIT):
    return pltpu.CompilerParams(dimension_semantics=sem, vmem_limit_bytes=vmem)


def _rms(xf, g):
    ms = jnp.mean(xf * xf, axis=-1, keepdims=True)
    return xf * lax.rsqrt(ms + EPS) * g


def _sigmoid(x):
    return 1.0 / (1.0 + jnp.exp(-x))


def _gelu_tanh(x):
    c = math.sqrt(2.0 / math.pi)
    return x * (0.5 * (1.0 + jnp.tanh(c * (x + 0.044715 * (x * x * x)))))


def _lambda_init(layer):
    return 0.8 - 0.6 * math.exp(-0.3 * layer)


def _alibi_slope(h):
    return 2.0 ** (-8.0 * (h + 1) / N_HEADS)


def _norm_matmul_kernel(x_ref, g_ref, w_ref, bgate_ref, o_ref, gate_ref, h_scr, *, n_main):
    j = pl.program_id(1)

    @pl.when(j == 0)
    def _():
        h_scr[...] = _rms(x_ref[...], g_ref[...]).astype(BF16)

    @pl.when(j < n_main)
    def _():
        o_ref[...] = jnp.dot(h_scr[...], w_ref[...], preferred_element_type=F32)

    @pl.when(j >= n_main)
    def _():
        cw = V7X_MXU_DIM
        for c in range(w_ref.shape[1] // cw):
            cols = slice(c * cw, (c + 1) * cw)
            logits = jnp.dot(h_scr[...], w_ref[:, cols], preferred_element_type=F32)
            gate_ref[:, cols] = _sigmoid(logits + bgate_ref[:, cols]).astype(BF16)


def norm_matmul(x, g, w_all, b_gate_row, layer, tm, tn):
    m, d = x.shape
    n = w_all.shape[2]
    n_main = COL_GATE * D_QK // tn
    n_gate = b_gate_row.shape[1] // tn
    assert n_main + n_gate == n // tn
    return pl.pallas_call(
        functools.partial(_norm_matmul_kernel, n_main=n_main),
        grid=(m // tm, n // tn),
        in_specs=[
            pl.BlockSpec((tm, d), lambda i, j: (i, 0)),
            pl.BlockSpec((1, d), lambda i, j: (0, 0)),
            pl.BlockSpec((None, d, tn), lambda i, j: (layer, 0, j)),
            pl.BlockSpec((1, tn), lambda i, j: (0, jnp.maximum(j - n_main, 0))),
        ],
        out_specs=[
            pl.BlockSpec((tm, tn), lambda i, j: (i, jnp.minimum(j, n_main - 1))),
            pl.BlockSpec((tm, tn), lambda i, j: (i, jnp.maximum(j - n_main, 0))),
        ],
        out_shape=[jax.ShapeDtypeStruct((m, n_main * tn), F32),
                   jax.ShapeDtypeStruct((m, n_gate * tn), BF16)],
        scratch_shapes=[pltpu.VMEM((tm, d), BF16)],
        compiler_params=_params(("arbitrary", "arbitrary")),
        name="norm_in_proj",
    )(x, g, w_all, b_gate_row)


def _ssm_prep_kernel(are_ref, aim_ref, ldt_ref, bre_ref, bim_ref, cre_ref, cim_ref,
                     abre_ref, abim_ref, bc_ref, cc_ref):
    a_re = are_ref[...]
    a_im = aim_ref[...]
    dt = jnp.exp(ldt_ref[...])
    mag = jnp.exp(dt * a_re)
    ab_re = mag * jnp.cos(dt * a_im)
    ab_im = mag * jnp.sin(dt * a_im)
    n_re = ab_re - 1.0
    n_im = ab_im
    den = a_re * a_re + a_im * a_im
    c_re = (n_re * a_re + n_im * a_im) / den
    c_im = (n_im * a_re - n_re * a_im) / den
    abre_ref[...] = ab_re
    abim_ref[...] = ab_im
    b_re = bre_ref[...]
    b_im = bim_ref[...]
    ns = b_re.shape[1]
    bc_ref[:, :ns] = (c_re * b_re - c_im * b_im).astype(BF16)
    bc_ref[:, ns:] = (c_re * b_im + c_im * b_re).astype(BF16)
    cc_ref[:ns, :] = cre_ref[...].astype(BF16)
    cc_ref[ns:, :] = (-cim_ref[...]).astype(BF16)


def ssm_prep(a_re, a_im, log_dt, b_re, b_im, c_re, c_im):
    g, p = a_re.shape
    kt = g // SSM_GROUPS_PER_TILE
    ns = SSM_STATE_PER_TILE
    eye = jnp.eye(SSM_GROUPS_PER_TILE, dtype=F32)

    def place_b(b):
        bt = b.reshape(kt, SSM_GROUPS_PER_TILE, p, SSM_GROUP).transpose(0, 1, 3, 2)
        return jnp.einsum('kgcp,gh->kgchp', bt, eye).reshape(kt, SSM_KTILE, ns)

    def place_c(c):
        ct = c.reshape(kt, SSM_GROUPS_PER_TILE, SSM_GROUP, p).transpose(0, 1, 3, 2)
        return jnp.einsum('kgpc,gh->kgphc', ct, eye).reshape(kt, ns, SSM_KTILE)

    row = lambda a: a.reshape(1, g * p)
    ldt = row(jnp.broadcast_to(log_dt[:, None], (g, p)))
    vec = pl.BlockSpec((1, ns), lambda k: (0, k))
    bspec = pl.BlockSpec((None, SSM_KTILE, ns), lambda k: (k, 0, 0))
    cspec = pl.BlockSpec((None, ns, SSM_KTILE), lambda k: (k, 0, 0))
    return pl.pallas_call(
        _ssm_prep_kernel,
        grid=(kt,),
        in_specs=[vec, vec, vec, bspec, bspec, cspec, cspec],
        out_specs=[vec, vec,
                   pl.BlockSpec((None, SSM_KTILE, 2 * ns), lambda k: (k, 0, 0)),
                   pl.BlockSpec((None, 2 * ns, SSM_KTILE), lambda k: (k, 0, 0))],
        out_shape=[jax.ShapeDtypeStruct((1, g * p), F32),
                   jax.ShapeDtypeStruct((1, g * p), F32),
                   jax.ShapeDtypeStruct((kt, SSM_KTILE, 2 * ns), BF16),
                   jax.ShapeDtypeStruct((kt, 2 * ns, SSM_KTILE), BF16)],
        compiler_params=_params(("arbitrary",)),
        name="ssm_discretise",
    )(row(a_re), row(a_im), ldt, place_b(b_re), place_b(b_im), place_c(c_re), place_c(c_im))


SSM_SCAN_LANES = 256


def _ssm_kernel(*refs, nb, n_kt):
    u_refs = refs[:nb]
    (h0r_ref, h0i_ref, ar_ref, ai_ref, bc_ref, cc_ref, d_ref, wg_ref, bg_ref,
     out_ref, hr_ref, hi_ref, io_scr, hs_scr) = refs[nb:]
    ns = SSM_STATE_PER_TILE
    tc, d_ssm = u_refs[0].shape
    rows = nb * tc
    n_lt = d_ssm // LANES

    @pl.when(pl.program_id(0) == 0)
    def _():
        hr_ref[...] = h0r_ref[...]
        hi_ref[...] = h0i_ref[...]

    for b in range(nb):
        for j in range(n_lt):
            io_scr[j, pl.ds(b, tc, stride=nb), :] = u_refs[b][:, j * LANES:(j + 1) * LANES]
    u = jnp.concatenate([io_scr[j] for j in range(n_lt)], axis=1)
    ub = u.astype(BF16)
    def input_map(kt):
        hs_scr[:, kt * 2 * ns:(kt + 1) * 2 * ns] = jnp.dot(
            ub[:, kt * SSM_KTILE:(kt + 1) * SSM_KTILE], bc_ref[kt], preferred_element_type=F32)

    rpi = hr_ref.shape[0]
    two_step = rpi != nb
    n_iter = rows // rpi
    w = SSM_SCAN_LANES

    def recurrence(kt):
        for s in range(ns // w):
            re0 = kt * 2 * ns + s * w
            im0 = re0 + ns
            a0 = kt * ns + s * w
            ar = jnp.broadcast_to(ar_ref[:, a0:a0 + w], (rpi, w))
            ai = jnp.broadcast_to(ai_ref[:, a0:a0 + w], (rpi, w))
            hr = hr_ref[:, a0:a0 + w]
            hi = hi_ref[:, a0:a0 + w]
            for it in range(n_iter):
                r0 = it * rpi
                br = hs_scr[r0:r0 + rpi, re0:re0 + w]
                bi = hs_scr[r0:r0 + rpi, im0:im0 + w]
                nr = ar * hr - ai * hi + br
                ni = ar * hi + ai * hr + bi
                if two_step:
                    sr = pltpu.roll(nr, nb, 0)
                    si = pltpu.roll(ni, nb, 0)
                    mr = ar * sr - ai * si + br
                    mi = ar * si + ai * sr + bi
                    hs_scr[r0:r0 + nb, re0:re0 + w] = nr[:nb]
                    hs_scr[r0 + nb:r0 + rpi, re0:re0 + w] = mr[nb:]
                    hs_scr[r0:r0 + nb, im0:im0 + w] = ni[:nb]
                    hs_scr[r0 + nb:r0 + rpi, im0:im0 + w] = mi[nb:]
                    hr = pltpu.roll(mr, nb, 0)
                    hi = pltpu.roll(mi, nb, 0)
                else:
                    hs_scr[r0:r0 + rpi, re0:re0 + w] = nr
                    hs_scr[r0:r0 + rpi, im0:im0 + w] = ni
                    hr, hi = nr, ni
            hr_ref[:, a0:a0 + w] = hr
            hi_ref[:, a0:a0 + w] = hi

    cols = []
    input_map(0)
    for kt in range(n_kt):
        if kt + 1 < n_kt:
            input_map(kt + 1)
        recurrence(kt)
        hk = hs_scr[:, kt * 2 * ns:(kt + 1) * 2 * ns].astype(BF16)
        cols.append(jnp.dot(hk, cc_ref[kt], preferred_element_type=F32))
    y = jnp.concatenate(cols, axis=1) + d_ref[...] * u
    z = _gelu_tanh(y)
    gate = _sigmoid(jnp.dot(z.astype(BF16), wg_ref[...], preferred_element_type=F32) + bg_ref[...])
    out = z * gate
    for j in range(n_lt):
        io_scr[j] = out[:, j * LANES:(j + 1) * LANES]
    for b in range(nb):
        out_ref[b] = jnp.concatenate(
            [io_scr[j, pl.ds(b, tc, stride=nb), :] for j in range(n_lt)], axis=1).astype(BF16)


def ssm_branch(proj, row0, seq, nb, tc, h0_re, h0_im, ab_re, ab_im, bc, cc, d_row, w_glu_all,
               layer, b_glu_row):
    d_ssm = d_row.shape[1]
    n_kt = d_ssm // SSM_KTILE
    n_state = ab_re.shape[1]
    full = lambda a: pl.BlockSpec(a.shape, lambda c: (0,) * a.ndim)
    srows = max(nb, V7X_SUBLANES)
    assert srows == nb or srows == 2 * nb
    if srows != nb:
        h0_re = jnp.concatenate([h0_re, h0_re], axis=0)
        h0_im = jnp.concatenate([h0_im, h0_im], axis=0)
    n_chunks = seq // tc
    rb0 = row0 // tc
    u_specs = [pl.BlockSpec((tc, d_ssm), lambda c, b=b: (rb0 + b * n_chunks + c, 0))
               for b in range(nb)]
    a_out, h_re, h_im = pl.pallas_call(
        functools.partial(_ssm_kernel, nb=nb, n_kt=n_kt),
        grid=(n_chunks,),
        in_specs=u_specs + [
            full(h0_re), full(h0_im), full(ab_re), full(ab_im), full(bc), full(cc), full(d_row),
            pl.BlockSpec((None, d_ssm, d_ssm), lambda c: (layer, 0, 0)),
            full(b_glu_row),
        ],
        out_specs=[
            pl.BlockSpec((nb, tc, d_ssm), lambda c: (0, c, 0)),
            pl.BlockSpec((srows, n_state), lambda c: (0, 0)),
            pl.BlockSpec((srows, n_state), lambda c: (0, 0)),
        ],
        out_shape=[
            jax.ShapeDtypeStruct((nb, seq, d_ssm), BF16),
            jax.ShapeDtypeStruct((srows, n_state), F32),
            jax.ShapeDtypeStruct((srows, n_state), F32),
        ],
        scratch_shapes=[pltpu.VMEM((d_ssm // LANES, nb * tc, LANES), F32),
                        pltpu.VMEM((nb * tc, 2 * n_state), F32)],
        compiler_params=_params(("arbitrary",)),
        name="ssm_scan",
    )(*([proj] * nb), h0_re, h0_im, ab_re, ab_im, bc, cc, d_row, w_glu_all, b_glu_row)
    return a_out.reshape(nb * seq, d_ssm), h_re[:nb], h_im[:nb]


def _diff_lambda(lq1_ref, lk1_ref, lq2_ref, lk2_ref, lam_init):
    s1 = jnp.sum(lq1_ref[...] * lk1_ref[...], axis=-1, keepdims=True)
    s2 = jnp.sum(lq2_ref[...] * lk2_ref[...], axis=-1, keepdims=True)
    return jnp.exp(s1) - jnp.exp(s2) + lam_init


def _masked_distance(qpos, kpos):
    dist = jnp.abs(qpos - kpos).astype(F32)
    shift = CHUNK.bit_length() - 1
    assert CHUNK == 1 << shift
    allowed = jnp.right_shift(kpos, shift) <= jnp.right_shift(qpos, shift)
    return jnp.where(allowed, dist, jnp.inf)


def _softmax_step(hm, s, m_scr, l_scr):
    m_old = m_scr[hm]
    m_new = jnp.maximum(m_old, jnp.max(s, axis=-1, keepdims=True))
    alpha = jnp.exp(m_old - m_new)
    p = jnp.exp(s - m_new)
    l_scr[hm] = alpha * l_scr[hm] + jnp.sum(p, axis=-1, keepdims=True)
    m_scr[hm] = m_new
    return p.astype(BF16), alpha


def _attn_tile(q_scr, k_tile, v_tile, distm, m_scr, l_scr, acc_scr):
    scale = HEAD_DIM ** -0.5
    rows = q_scr.shape[0]
    for h in range(N_HEADS):
        bias = (-_alibi_slope(h)) * distm
        ps, alphas = [], []
        for mp in range(2):
            hm = 2 * h + mp
            q = q_scr[:, hm * HEAD_DIM:(hm + 1) * HEAD_DIM]
            s = lax.dot_general(q, k_tile(hm), (((1,), (1,)), ((), ())),
                                preferred_element_type=F32) * scale + bias
            p, alpha = _softmax_step(hm, s, m_scr, l_scr)
            ps.append(p)
            alphas.append(alpha)
        pv = jnp.dot(jnp.concatenate(ps, axis=0), v_tile(h), preferred_element_type=F32)
        for mp in range(2):
            hm = 2 * h + mp
            acc_scr[hm] = alphas[mp] * acc_scr[hm] + pv[mp * rows:(mp + 1) * rows]


def _attn_init(m_scr, l_scr, acc_scr):
    m_scr[...] = jnp.full(m_scr.shape, -jnp.inf, F32)
    l_scr[...] = jnp.zeros(l_scr.shape, F32)
    acc_scr[...] = jnp.zeros(acc_scr.shape, F32)


def _attn_finish(o_ref, lam, gsub_ref, lam_init, m_scr, l_scr, acc_scr):
    for h in range(N_HEADS):
        o = acc_scr[2 * h] / l_scr[2 * h] - lam * (acc_scr[2 * h + 1] / l_scr[2 * h + 1])
        o = _rms(o, gsub_ref[...]) * (1.0 - lam_init)
        o_ref[:, h * V_DIM:(h + 1) * V_DIM] = o.astype(BF16)


def _fold_lanes(x, op):
    out = x[:, :LANES]
    for c in range(1, x.shape[1] // LANES):
        out = op(out, x[:, c * LANES:(c + 1) * LANES])
    return out


def _attn_prompt_kernel(q_ref, k_ref, v_ref, lq1_ref, lk1_ref, lq2_ref, lk2_ref, gsub_ref,
                        o_ref, q_scr, k_scr, v_scr, s_scr, m_scr, l_scr, acc_scr, *, lam_init):
    i = pl.program_id(1)
    tq = q_ref.shape[0]
    scale = HEAD_DIM ** -0.5

    @pl.when(i == 0)
    def _():
        k_scr[...] = k_ref[...].astype(BF16)
        v_scr[...] = v_ref[...].astype(BF16)

    q_scr[...] = q_ref[...].astype(BF16)
    row = lax.broadcasted_iota(jnp.int32, (tq, tq), 0)
    col = lax.broadcasted_iota(jnp.int32, (tq, tq), 1)
    rel = (row - col).astype(F32)
    dist_diag = _masked_distance(i * tq + row, i * tq + col)
    lam = _diff_lambda(lq1_ref, lk1_ref, lq2_ref, lk2_ref, lam_init)

    def scores(h, mp, k0, bias):
        hm = 2 * h + mp
        q = q_scr[:, hm * HEAD_DIM:(hm + 1) * HEAD_DIM]
        k = k_scr[pl.ds(k0, tq), hm * HEAD_DIM:(hm + 1) * HEAD_DIM]
        return lax.dot_general(q, k, (((1,), (1,)), ((), ())),
                               preferred_element_type=F32) * scale + bias

    d0 = pl.multiple_of(i * tq, tq)

    def diag_scores(h):
        slot = h % 2
        bias = (-_alibi_slope(h)) * dist_diag
        for mp in range(2):
            s = scores(h, mp, d0, bias)
            s_scr[slot, mp, :, pl.ds(d0, tq)] = s
            m_scr[slot, mp] = _fold_lanes(s, jnp.maximum)

    def pass1(h, j):
        slot = h % 2
        neg_slope = -_alibi_slope(h)
        k0 = pl.multiple_of(j * tq, tq)
        bias = neg_slope * rel + neg_slope * ((i - j) * tq).astype(F32)
        for mp in range(2):
            s = scores(h, mp, k0, bias)
            s_scr[slot, mp, :, pl.ds(k0, tq)] = s
            m_scr[slot, mp] = jnp.maximum(m_scr[slot, mp], _fold_lanes(s, jnp.maximum))

    def pass2(h, k0, m_row):
        slot = h % 2
        v = v_scr[pl.ds(k0, tq), h * V_DIM:(h + 1) * V_DIM]
        for mp in range(2):
            p = jnp.exp(s_scr[slot, mp, :, pl.ds(k0, tq)] - m_row[mp])
            l_scr[mp] += _fold_lanes(p, jnp.add)
            acc_scr[mp] += jnp.dot(p.astype(BF16), v, preferred_element_type=F32)

    diag_scores(0)
    lax.fori_loop(0, i, lambda j, c: (pass1(0, j), c)[1], 0)
    for h in range(N_HEADS):
        slot = h % 2
        m_row = [jnp.max(m_scr[slot, mp], axis=-1, keepdims=True) for mp in range(2)]
        l_scr[...] = jnp.zeros(l_scr.shape, F32)
        acc_scr[...] = jnp.zeros(acc_scr.shape, F32)
        nxt = h + 1 if h + 1 < N_HEADS else None
        if nxt is not None:
            diag_scores(nxt)

        def both(j, carry, h=h, nxt=nxt, m_row=m_row):
            pass2(h, pl.multiple_of(j * tq, tq), m_row)
            if nxt is not None:
                pass1(nxt, j)
            return carry

        lax.fori_loop(0, i, both, 0)
        pass2(h, d0, m_row)
        l_row = [jnp.sum(l_scr[mp], axis=-1, keepdims=True) for mp in range(2)]
        o = acc_scr[0] / l_row[0] - lam * (acc_scr[1] / l_row[1])
        o = _rms(o, gsub_ref[...]) * (1.0 - lam_init)
        o_ref[:, h * V_DIM:(h + 1) * V_DIM] = o.astype(BF16)


def attn_prompt(proj, seq, batch, lam_rows, gsub_row, lam_init, tq):
    rows = proj.shape[0]
    nq = seq // tq
    lam_spec = pl.BlockSpec((1, HEAD_DIM), lambda b, i: (0, 0))
    return pl.pallas_call(
        functools.partial(_attn_prompt_kernel, lam_init=lam_init),
        grid=(batch, nq),
        in_specs=[
            pl.BlockSpec((tq, D_QK), lambda b, i: (b * nq + i, COL_Q)),
            pl.BlockSpec((seq, D_QK), lambda b, i: (b, COL_K), pipeline_mode=pl.Buffered(1)),
            pl.BlockSpec((seq, D_ATTN), lambda b, i: (b, COL_V), pipeline_mode=pl.Buffered(1)),
            lam_spec, lam_spec, lam_spec, lam_spec,
            pl.BlockSpec((1, V_DIM), lambda b, i: (0, 0)),
        ],
        out_specs=pl.BlockSpec((tq, D_ATTN), lambda b, i: (b * nq + i, 0)),
        out_shape=jax.ShapeDtypeStruct((rows, D_ATTN), BF16),
        scratch_shapes=[
            pltpu.VMEM((tq, D_QK), BF16),
            pltpu.VMEM((seq, D_QK), BF16),
            pltpu.VMEM((seq, D_ATTN), BF16),
            pltpu.VMEM((2, 2, tq, seq), F32),
            pltpu.VMEM((2, 2, tq, LANES), F32),
            pltpu.VMEM((2, tq, LANES), F32),
            pltpu.VMEM((2, tq, V_DIM), F32),
        ],
        compiler_params=_params(("arbitrary", "arbitrary")),
        name="attn_prompt",
    )(proj, proj, proj, *lam_rows, gsub_row)


def _attn_sample_kernel(q_ref, kn_ref, vn_ref, kp_ref, vp_ref, lq1_ref, lk1_ref, lq2_ref, lk2_ref,
                        gsub_ref, o_prev_ref, o_ref, q_scr, m_scr, l_scr, acc_scr, *, lam_init, past):
    del o_prev_ref
    j = pl.program_id(1)
    s_len = q_ref.shape[0]
    tk = kp_ref.shape[0] // CACHE_ROWS

    @pl.when(j == 0)
    def _():
        q_scr[...] = q_ref[...].astype(BF16)
        _attn_init(m_scr, l_scr, acc_scr)

    def v_past(h):
        halves = [vp_ref[pl.ds(c * N_HEADS + h, tk, stride=CACHE_ROWS), :] for c in range(2)]
        return jnp.concatenate(halves, axis=1).astype(BF16)

    qpos = past + lax.broadcasted_iota(jnp.int32, (s_len, tk), 0)
    kpos = j * tk + lax.broadcasted_iota(jnp.int32, (s_len, tk), 1)
    _attn_tile(
        q_scr,
        lambda hm: kp_ref[pl.ds(hm, tk, stride=CACHE_ROWS), :].astype(BF16),
        v_past,
        _masked_distance(qpos, kpos), m_scr, l_scr, acc_scr)

    @pl.when(j == pl.num_programs(1) - 1)
    def _():
        qn = past + lax.broadcasted_iota(jnp.int32, (s_len, s_len), 0)
        kn = past + lax.broadcasted_iota(jnp.int32, (s_len, s_len), 1)
        _attn_tile(
            q_scr,
            lambda hm: kn_ref[:, hm * HEAD_DIM:(hm + 1) * HEAD_DIM].astype(BF16),
            lambda h: vn_ref[:, h * V_DIM:(h + 1) * V_DIM].astype(BF16),
            _masked_distance(qn, kn), m_scr, l_scr, acc_scr)
        lam = _diff_lambda(lq1_ref, lk1_ref, lq2_ref, lk2_ref, lam_init)
        _attn_finish(o_ref, lam, gsub_ref, lam_init, m_scr, l_scr, acc_scr)


def attn_sample(proj, o_prev, row0, s_len, batch, cache_k, cache_v, layer, lam_rows, gsub_row,
                lam_init, tk):
    rows = proj.shape[0]
    depth, _, past = cache_k.shape[:3]
    rb = row0 // s_len
    lam_spec = pl.BlockSpec((1, HEAD_DIM), lambda b, j: (0, 0))
    k_rows = cache_k.reshape(depth, batch, past * CACHE_ROWS, HEAD_DIM)
    v_rows = (cache_v.reshape(depth, batch, past, N_HEADS, 2, HEAD_DIM)
              .transpose(0, 1, 2, 4, 3, 5).reshape(depth, batch, past * CACHE_ROWS, HEAD_DIM))
    cache_spec = pl.BlockSpec((None, None, tk * CACHE_ROWS, HEAD_DIM), lambda b, j: (layer, b, j, 0))
    return pl.pallas_call(
        functools.partial(_attn_sample_kernel, lam_init=lam_init, past=past),
        grid=(batch, past // tk),
        in_specs=[
            pl.BlockSpec((s_len, D_QK), lambda b, j: (rb + b, COL_Q)),
            pl.BlockSpec((s_len, D_QK), lambda b, j: (rb + b, COL_K)),
            pl.BlockSpec((s_len, D_ATTN), lambda b, j: (rb + b, COL_V)),
            cache_spec, cache_spec,
            lam_spec, lam_spec, lam_spec, lam_spec,
            pl.BlockSpec((1, V_DIM), lambda b, j: (0, 0)),
            pl.BlockSpec(memory_space=pl.ANY),
        ],
        out_specs=pl.BlockSpec((s_len, D_ATTN), lambda b, j: (rb + b, 0)),
        out_shape=jax.ShapeDtypeStruct((rows, D_ATTN), BF16),
        input_output_aliases={10: 0},
        scratch_shapes=[
            pltpu.VMEM((s_len, D_QK), BF16),
            pltpu.VMEM((2 * N_HEADS, s_len, 1), F32),
            pltpu.VMEM((2 * N_HEADS, s_len, 1), F32),
            pltpu.VMEM((2 * N_HEADS, s_len, V_DIM), F32),
        ],
        compiler_params=_params(("arbitrary", "arbitrary")),
        name="attn_sample",
    )(proj, proj, proj, k_rows, v_rows, *lam_rows, gsub_row, o_prev)


def _kv_export_kernel(k_ref, v_ref, *refs):
    ko_ref, vo_ref = refs[-2:]
    for h in range(N_HEADS):
        vo_ref[:, h, :] = v_ref[:, h * V_DIM:(h + 1) * V_DIM]
        for mp in range(2):
            hm = 2 * h + mp
            ko_ref[:, h, mp, :] = k_ref[:, hm * HEAD_DIM:(hm + 1) * HEAD_DIM]


def kv_export(proj, row0, seq, batch, depth, layer, k_prev, v_prev, tr):
    rb = row0 // tr
    nr = seq // tr
    in_specs = [
        pl.BlockSpec((tr, D_QK), lambda b, i: (rb + b * nr + i, COL_K)),
        pl.BlockSpec((tr, D_ATTN), lambda b, i: (rb + b * nr + i, COL_V)),
    ]
    args = [proj, proj]
    aliases = {}
    if k_prev is not None:
        in_specs += [pl.BlockSpec(memory_space=pl.ANY), pl.BlockSpec(memory_space=pl.ANY)]
        args += [k_prev, v_prev]
        aliases = {2: 0, 3: 1}
    return pl.pallas_call(
        _kv_export_kernel,
        grid=(batch, seq // tr),
        in_specs=in_specs,
        out_specs=[
            pl.BlockSpec((None, None, tr, N_HEADS, 2, HEAD_DIM), lambda b, i: (layer, b, i, 0, 0, 0)),
            pl.BlockSpec((None, None, tr, N_HEADS, V_DIM), lambda b, i: (layer, b, i, 0, 0)),
        ],
        out_shape=[
            jax.ShapeDtypeStruct((depth, batch, seq, N_HEADS, 2, HEAD_DIM), F32),
            jax.ShapeDtypeStruct((depth, batch, seq, N_HEADS, V_DIM), F32),
        ],
        input_output_aliases=aliases,
        compiler_params=_params(("arbitrary", "arbitrary")),
        name="kv_export",
    )(*args)


def _merge_kernel(a_ref, o_ref, wa_ref, wo_ref, ga_ref, go_ref, out_ref):
    ya = jnp.dot(a_ref[...], wa_ref[...], preferred_element_type=F32)
    yo = jnp.dot(o_ref[...], wo_ref[...], preferred_element_type=F32)
    merged = ga_ref[...].astype(F32) * ya + go_ref[...].astype(F32) * yo
    out_ref[...] = merged.astype(BF16)


def gated_merge(a_out, o_out, w_ssm_all, w_attn_all, gates, layer, tm, tn):
    m, d_br = a_out.shape
    d_model = w_ssm_all.shape[2]
    nj = d_model // tn
    return pl.pallas_call(
        _merge_kernel,
        grid=(m // tm, nj),
        in_specs=[
            pl.BlockSpec((tm, d_br), lambda i, j: (i, 0)),
            pl.BlockSpec((tm, d_br), lambda i, j: (i, 0)),
            pl.BlockSpec((None, d_br, tn), lambda i, j: (layer, 0, j)),
            pl.BlockSpec((None, d_br, tn), lambda i, j: (layer, 0, j)),
            pl.BlockSpec((tm, tn), lambda i, j: (i, j)),
            pl.BlockSpec((tm, tn), lambda i, j: (i, nj + j)),
        ],
        out_specs=pl.BlockSpec((tm, tn), lambda i, j: (i, j)),
        out_shape=jax.ShapeDtypeStruct((m, d_model), BF16),
        compiler_params=_params(("parallel", "arbitrary")),
        name="gated_merge",
    )(a_out, o_out, w_ssm_all, w_attn_all, gates, gates)


def _out_proj_kernel(a_ref, w_ref, x_ref, g_ref, o_ref, *, tn):
    n = o_ref.shape[1]
    a = a_ref[...]
    ssq = jnp.zeros((a.shape[0], 1), F32)
    for c in range(n // tn):
        y = jnp.dot(a, w_ref[:, c * tn:(c + 1) * tn], preferred_element_type=F32)
        ssq = ssq + jnp.sum(y * y, axis=-1, keepdims=True)
        o_ref[:, c * tn:(c + 1) * tn] = y
    inv = lax.rsqrt(ssq / n + EPS)
    o_ref[...] = x_ref[...] + o_ref[...] * inv * g_ref[...]


def out_proj_residual(merged, w_all, x, g_row, layer, tm, tn):
    m, d = x.shape
    k = merged.shape[1]
    return pl.pallas_call(
        functools.partial(_out_proj_kernel, tn=tn),
        grid=(m // tm,),
        in_specs=[
            pl.BlockSpec((tm, k), lambda i: (i, 0)),
            pl.BlockSpec((None, k, d), lambda i: (layer, 0, 0)),
            pl.BlockSpec((tm, d), lambda i: (i, 0)),
            pl.BlockSpec((1, d), lambda i: (0, 0)),
        ],
        out_specs=pl.BlockSpec((tm, d), lambda i: (i, 0)),
        out_shape=jax.ShapeDtypeStruct((m, d), F32),
        compiler_params=_params(("parallel",)),
        name="out_proj_residual",
    )(merged, w_all, x, g_row)


def _ffn_kernel(x_ref, gpre_ref, wg_ref, wu_ref, wd_ref, gpost_ref, o_ref, h_scr):
    j = pl.program_id(1)

    @pl.when(j == 0)
    def _():
        h_scr[...] = _rms(x_ref[...], gpre_ref[...]).astype(BF16)
        o_ref[...] = jnp.zeros(o_ref.shape, F32)

    h = h_scr[...]
    gate = jnp.dot(h, wg_ref[...], preferred_element_type=F32)
    up = jnp.dot(h, wu_ref[...], preferred_element_type=F32)
    act = (gate * _sigmoid(gate) * up).astype(BF16)
    o_ref[...] += jnp.dot(act, wd_ref[...], preferred_element_type=F32)

    @pl.when(j == pl.num_programs(1) - 1)
    def _():
        o_ref[...] = x_ref[...] + _rms(o_ref[...], gpost_ref[...])


def ffn_residual(x, gpre_row, wg_all, wu_all, wd_all, gpost_row, layer, tm, tf):
    m, d = x.shape
    f = wg_all.shape[2]
    return pl.pallas_call(
        _ffn_kernel,
        grid=(m // tm, f // tf),
        in_specs=[
            pl.BlockSpec((tm, d), lambda i, j: (i, 0)),
            pl.BlockSpec((1, d), lambda i, j: (0, 0)),
            pl.BlockSpec((None, d, tf), lambda i, j: (layer, 0, j)),
            pl.BlockSpec((None, d, tf), lambda i, j: (layer, 0, j)),
            pl.BlockSpec((None, tf, d), lambda i, j: (layer, j, 0)),
            pl.BlockSpec((1, d), lambda i, j: (0, 0)),
        ],
        out_specs=pl.BlockSpec((tm, d), lambda i, j: (i, 0)),
        out_shape=jax.ShapeDtypeStruct((m, d), F32),
        scratch_shapes=[pltpu.VMEM((tm, d), BF16)],
        compiler_params=_params(("parallel", "arbitrary")),
        name="swiglu_residual",
    )(x, gpre_row, wg_all, wu_all, wd_all, gpost_row)


def _tile(n, want):
    t = min(n, want)
    assert n % t == 0, (n, want)
    return t


def kernel(x_prompt, x_sample, cache_k, cache_v, state_ssm_re, state_ssm_im, g_pre_mix, w_in, b_gate, ssm_a_re, ssm_a_im, ssm_log_dt, ssm_b_re, ssm_b_im, ssm_c_re, ssm_c_im, ssm_d, w_glu, b_glu, lam_q1, lam_k1, lam_q2, lam_k2, g_sub, w_br_ssm, w_br_attn, w_out, g_post_mix, g_pre_ffn, w_ffn_gate, w_ffn_up, w_ffn_down, g_post_ffn):
    batch, seq, d_model = x_prompt.shape
    dec_batch, dec_seq, _ = x_sample.shape
    depth = w_in.shape[0]
    past = cache_k.shape[2]
    n_groups, state_dim = ssm_a_re.shape[1:]
    n_state = n_groups * state_dim
    assert ssm_d.shape[1] == D_SSM and w_in.shape[2] == COL_GATE * D_QK + N_BRANCH * d_model
    assert w_in.shape[2] % D_QK == 0
    rows_p = seq * batch
    rows_s = dec_seq * dec_batch

    x = jnp.concatenate([x_prompt.reshape(rows_p, d_model),
                         x_sample.reshape(rows_s, d_model)], axis=0)

    w_in_b = w_in.astype(BF16)
    w_glu_b = w_glu.astype(BF16)
    w_br_ssm_b = w_br_ssm.astype(BF16)
    w_br_attn_b = w_br_attn.astype(BF16)
    w_out_b = w_out.astype(BF16)
    w_gate_b = w_ffn_gate.astype(BF16)
    w_up_b = w_ffn_up.astype(BF16)
    w_down_b = w_ffn_down.astype(BF16)

    tm = _tile(rows_s, 1024)
    assert rows_p % tm == 0
    ssm_rows = 256
    zeros_state = jnp.zeros((batch, n_state), F32)

    hr_p, hi_p, hr_s, hi_s = [], [], [], []
    k_p = v_p = k_s = v_s = None
    for l in range(depth):
        lam_init = _lambda_init(l)
        proj, gates = norm_matmul(x, g_pre_mix[l][None], w_in_b,
                                  b_gate[l].reshape(1, N_BRANCH * d_model), l, tm,
                                  _tile(N_BRANCH * d_model, 1024))

        ab_re, ab_im, bc, cc = ssm_prep(ssm_a_re[l], ssm_a_im[l], ssm_log_dt[l], ssm_b_re[l],
                                        ssm_b_im[l], ssm_c_re[l], ssm_c_im[l])
        d_row = ssm_d[l][None]
        bg_row = b_glu[l][None]
        a_p, hrp, hip = ssm_branch(proj, 0, seq, batch, ssm_rows // batch, zeros_state, zeros_state,
                                   ab_re, ab_im, bc, cc, d_row, w_glu_b, l, bg_row)
        a_s, hrs, his = ssm_branch(proj, rows_p, dec_seq, dec_batch, ssm_rows // dec_batch,
                                   state_ssm_re[l].reshape(dec_batch, n_state),
                                   state_ssm_im[l].reshape(dec_batch, n_state),
                                   ab_re, ab_im, bc, cc, d_row, w_glu_b, l, bg_row)
        a_out = jnp.concatenate([a_p, a_s], axis=0)

        lam_rows = (lam_q1[l][None], lam_k1[l][None], lam_q2[l][None], lam_k2[l][None])
        gsub_row = g_sub[l][None]
        o_out = attn_prompt(proj, seq, batch, lam_rows, gsub_row, lam_init, _tile(seq, 512))
        o_out = attn_sample(proj, o_out, rows_p, dec_seq, dec_batch, cache_k, cache_v, l, lam_rows,
                            gsub_row, lam_init, _tile(past, 2048))
        k_p, v_p = kv_export(proj, 0, seq, batch, depth, l, k_p, v_p, _tile(seq, 512))
        k_s, v_s = kv_export(proj, rows_p, dec_seq, dec_batch, depth, l, k_s, v_s, dec_seq)

        merged = gated_merge(a_out, o_out, w_br_ssm_b, w_br_attn_b, gates, l, tm,
                             _tile(d_model, 1024))
        x = out_proj_residual(merged, w_out_b, x, g_post_mix[l][None], l, _tile(rows_s, 512),
                              _tile(d_model, 512))
        x = ffn_residual(x, g_pre_ffn[l][None], w_gate_b, w_up_b, w_down_b, g_post_ffn[l][None],
                         l, _tile(rows_s, 512), _tile(w_ffn_gate.shape[2], 512))

        hr_p.append(hrp.reshape(batch, n_groups, state_dim))
        hi_p.append(hip.reshape(batch, n_groups, state_dim))
        hr_s.append(hrs.reshape(dec_batch, n_groups, state_dim))
        hi_s.append(his.reshape(dec_batch, n_groups, state_dim))

    y_p = x[:rows_p].reshape(batch, seq, d_model)
    y_s = x[rows_p:].reshape(dec_batch, dec_seq, d_model)
    return (y_p, y_s, k_p, v_p, jnp.stack(hr_p), jnp.stack(hi_p),
            k_s, v_s, jnp.stack(hr_s), jnp.stack(hi_s))
```

```python
import functools
import math

import jax
import jax.numpy as jnp
from jax import lax
from jax.experimental import pallas as pl
from jax.experimental.pallas import tpu as pltpu

F32 = jnp.float32
BF16 = jnp.bfloat16

CHUNK = 64
SSM_GROUP = 16
STATE_DIM = 64
N_HEADS = 4
HEAD_DIM = 128
V_DIM = 2 * HEAD_DIM
N_BRANCH = 2
EPS = 1e-6
D_SSM = 1024
D_QK = N_HEADS * 2 * HEAD_DIM
D_ATTN = N_HEADS * V_DIM
assert D_SSM == D_QK == D_ATTN
COL_Q, COL_K, COL_V, COL_GATE = 1, 2, 3, 4
CACHE_ROWS = D_QK // HEAD_DIM

V7X_MXU_DIM = 256
V7X_SUBLANES = 8
LANES = 128
V7X_VMEM_LIMIT = 56 * 1024 * 1024

SSM_KTILE = V7X_MXU_DIM
SSM_GROUPS_PER_TILE = SSM_KTILE // SSM_GROUP
SSM_STATE_PER_TILE = SSM_GROUPS_PER_TILE * STATE_DIM


def _params(sem, vmem=V7X_VMEM_LIMIT):
    return pltpu.CompilerParams(dimension_semantics=sem, vmem_limit_bytes=vmem)


def _rms(xf, g):
    ms = jnp.mean(xf * xf, axis=-1, keepdims=True)
    return xf * lax.rsqrt(ms + EPS) * g


def _sigmoid(x):
    return 1.0 / (1.0 + jnp.exp(-x))


def _gelu_tanh(x):
    c = math.sqrt(2.0 / math.pi)
    return x * (0.5 * (1.0 + jnp.tanh(c * (x + 0.044715 * (x * x * x)))))


def _lambda_init(layer):
    return 0.8 - 0.6 * math.exp(-0.3 * layer)


def _alibi_slope(h):
    return 2.0 ** (-8.0 * (h + 1) / N_HEADS)


def _norm_matmul_kernel(x_ref, g_ref, w_ref, bgate_ref, o_ref, gate_ref, h_scr, *, n_main):
    j = pl.program_id(1)

    @pl.when(j == 0)
    def _():
        h_scr[...] = _rms(x_ref[...], g_ref[...]).astype(BF16)

    @pl.when(j < n_main)
    def _():
        o_ref[...] = jnp.dot(h_scr[...], w_ref[...], preferred_element_type=F32)

    @pl.when(j >= n_main)
    def _():
        cw = V7X_MXU_DIM
        for c in range(w_ref.shape[1] // cw):
            cols = slice(c * cw, (c + 1) * cw)
            logits = jnp.dot(h_scr[...], w_ref[:, cols], preferred_element_type=F32)
            gate_ref[:, cols] = _sigmoid(logits + bgate_ref[:, cols]).astype(BF16)


def norm_matmul(x, g, w_all, b_gate_row, layer, tm, tn):
    m, d = x.shape
    n = w_all.shape[2]
    n_main = COL_GATE * D_QK // tn
    n_gate = b_gate_row.shape[1] // tn
    assert n_main + n_gate == n // tn
    return pl.pallas_call(
        functools.partial(_norm_matmul_kernel, n_main=n_main),
        grid=(m // tm, n // tn),
        in_specs=[
            pl.BlockSpec((tm, d), lambda i, j: (i, 0)),
            pl.BlockSpec((1, d), lambda i, j: (0, 0)),
            pl.BlockSpec((None, d, tn), lambda i, j: (layer, 0, j)),
            pl.BlockSpec((1, tn), lambda i, j: (0, jnp.maximum(j - n_main, 0))),
        ],
        out_specs=[
            pl.BlockSpec((tm, tn), lambda i, j: (i, jnp.minimum(j, n_main - 1))),
            pl.BlockSpec((tm, tn), lambda i, j: (i, jnp.maximum(j - n_main, 0))),
        ],
        out_shape=[jax.ShapeDtypeStruct((m, n_main * tn), F32),
                   jax.ShapeDtypeStruct((m, n_gate * tn), BF16)],
        scratch_shapes=[pltpu.VMEM((tm, d), BF16)],
        compiler_params=_params(("arbitrary", "arbitrary")),
        name="norm_in_proj",
    )(x, g, w_all, b_gate_row)


def _ssm_prep_kernel(are_ref, aim_ref, ldt_ref, bre_ref, bim_ref, cre_ref, cim_ref,
                     abre_ref, abim_ref, bc_ref, cc_ref):
    a_re = are_ref[...]
    a_im = aim_ref[...]
    dt = jnp.exp(ldt_ref[...])
    mag = jnp.exp(dt * a_re)
    ab_re = mag * jnp.cos(dt * a_im)
    ab_im = mag * jnp.sin(dt * a_im)
    n_re = ab_re - 1.0
    n_im = ab_im
    den = a_re * a_re + a_im * a_im
    c_re = (n_re * a_re + n_im * a_im) / den
    c_im = (n_im * a_re - n_re * a_im) / den
    abre_ref[...] = ab_re
    abim_ref[...] = ab_im
    b_re = bre_ref[...]
    b_im = bim_ref[...]
    ns = b_re.shape[1]
    bc_ref[:, :ns] = (c_re * b_re - c_im * b_im).astype(BF16)
    bc_ref[:, ns:] = (c_re * b_im + c_im * b_re).astype(BF16)
    cc_ref[:ns, :] = cre_ref[...].astype(BF16)
    cc_ref[ns:, :] = (-cim_ref[...]).astype(BF16)


def ssm_prep(a_re, a_im, log_dt, b_re, b_im, c_re, c_im):
    g, p = a_re.shape
    kt = g // SSM_GROUPS_PER_TILE
    ns = SSM_STATE_PER_TILE
    eye = jnp.eye(SSM_GROUPS_PER_TILE, dtype=F32)

    def place_b(b):
        bt = b.reshape(kt, SSM_GROUPS_PER_TILE, p, SSM_GROUP).transpose(0, 1, 3, 2)
        return jnp.einsum('kgcp,gh->kgchp', bt, eye).reshape(kt, SSM_KTILE, ns)

    def place_c(c):
        ct = c.reshape(kt, SSM_GROUPS_PER_TILE, SSM_GROUP, p).transpose(0, 1, 3, 2)
        return jnp.einsum('kgpc,gh->kgphc', ct, eye).reshape(kt, ns, SSM_KTILE)

    row = lambda a: a.reshape(1, g * p)
    ldt = row(jnp.broadcast_to(log_dt[:, None], (g, p)))
    vec = pl.BlockSpec((1, ns), lambda k: (0, k))
    bspec = pl.BlockSpec((None, SSM_KTILE, ns), lambda k: (k, 0, 0))
    cspec = pl.BlockSpec((None, ns, SSM_KTILE), lambda k: (k, 0, 0))
    return pl.pallas_call(
        _ssm_prep_kernel,
        grid=(kt,),
        in_specs=[vec, vec, vec, bspec, bspec, cspec, cspec],
        out_specs=[vec, vec,
                   pl.BlockSpec((None, SSM_KTILE, 2 * ns), lambda k: (k, 0, 0)),
                   pl.BlockSpec((None, 2 * ns, SSM_KTILE), lambda k: (k, 0, 0))],
        out_shape=[jax.ShapeDtypeStruct((1, g * p), F32),
                   jax.ShapeDtypeStruct((1, g * p), F32),
                   jax.ShapeDtypeStruct((kt, SSM_KTILE, 2 * ns), BF16),
                   jax.ShapeDtypeStruct((kt, 2 * ns, SSM_KTILE), BF16)],
        compiler_params=_params(("arbitrary",)),
        name="ssm_discretise",
    )(row(a_re), row(a_im), ldt, place_b(b_re), place_b(b_im), place_c(c_re), place_c(c_im))


SSM_SCAN_LANES = 256


def _ssm_kernel(*refs, nb, n_kt):
    u_refs = refs[:nb]
    (h0r_ref, h0i_ref, ar_ref, ai_ref, bc_ref, cc_ref, d_ref, wg_ref, bg_ref,
     out_ref, hr_ref, hi_ref, io_scr, hs_scr) = refs[nb:]
    ns = SSM_STATE_PER_TILE
    tc, d_ssm = u_refs[0].shape
    rows = nb * tc
    n_lt = d_ssm // LANES

    @pl.when(pl.program_id(0) == 0)
    def _():
        hr_ref[...] = h0r_ref[...]
        hi_ref[...] = h0i_ref[...]

    for b in range(nb):
        for j in range(n_lt):
            io_scr[j, pl.ds(b, tc, stride=nb), :] = u_refs[b][:, j * LANES:(j + 1) * LANES]
    u = jnp.concatenate([io_scr[j] for j in range(n_lt)], axis=1)
    ub = u.astype(BF16)
    def input_map(kt):
        hs_scr[:, kt * 2 * ns:(kt + 1) * 2 * ns] = jnp.dot(
            ub[:, kt * SSM_KTILE:(kt + 1) * SSM_KTILE], bc_ref[kt], preferred_element_type=F32)

    rpi = hr_ref.shape[0]
    two_step = rpi != nb
    n_iter = rows // rpi
    w = SSM_SCAN_LANES

    def recurrence(kt):
        for s in range(ns // w):
            re0 = kt * 2 * ns + s * w
            im0 = re0 + ns
            a0 = kt * ns + s * w
            ar = jnp.broadcast_to(ar_ref[:, a0:a0 + w], (rpi, w))
            ai = jnp.broadcast_to(ai_ref[:, a0:a0 + w], (rpi, w))
            hr = hr_ref[:, a0:a0 + w]
            hi = hi_ref[:, a0:a0 + w]
            for it in range(n_iter):
                r0 = it * rpi
                br = hs_scr[r0:r0 + rpi, re0:re0 + w]
                bi = hs_scr[r0:r0 + rpi, im0:im0 + w]
                nr = ar * hr - ai * hi + br
                ni = ar * hi + ai * hr + bi
                if two_step:
                    sr = pltpu.roll(nr, nb, 0)
                    si = pltpu.roll(ni, nb, 0)
                    mr = ar * sr - ai * si + br
                    mi = ar * si + ai * sr + bi
                    hs_scr[r0:r0 + nb, re0:re0 + w] = nr[:nb]
                    hs_scr[r0 + nb:r0 + rpi, re0:re0 + w] = mr[nb:]
                    hs_scr[r0:r0 + nb, im0:im0 + w] = ni[:nb]
                    hs_scr[r0 + nb:r0 + rpi, im0:im0 + w] = mi[nb:]
                    hr = pltpu.roll(mr, nb, 0)
                    hi = pltpu.roll(mi, nb, 0)
                else:
                    hs_scr[r0:r0 + rpi, re0:re0 + w] = nr
                    hs_scr[r0:r0 + rpi, im0:im0 + w] = ni
                    hr, hi = nr, ni
            hr_ref[:, a0:a0 + w] = hr
            hi_ref[:, a0:a0 + w] = hi

    cols = []
    input_map(0)
    for kt in range(n_kt):
        if kt + 1 < n_kt:
            input_map(kt + 1)
        recurrence(kt)
        hk = hs_scr[:, kt * 2 * ns:(kt + 1) * 2 * ns].astype(BF16)
        cols.append(jnp.dot(hk, cc_ref[kt], preferred_element_type=F32))
    y = jnp.concatenate(cols, axis=1) + d_ref[...] * u
    z = _gelu_tanh(y)
    gate = _sigmoid(jnp.dot(z.astype(BF16), wg_ref[...], preferred_element_type=F32) + bg_ref[...])
    out = z * gate
    for j in range(n_lt):
        io_scr[j] = out[:, j * LANES:(j + 1) * LANES]
    for b in range(nb):
        out_ref[b] = jnp.concatenate(
            [io_scr[j, pl.ds(b, tc, stride=nb), :] for j in range(n_lt)], axis=1).astype(BF16)


def ssm_branch(proj, row0, seq, nb, tc, h0_re, h0_im, ab_re, ab_im, bc, cc, d_row, w_glu_all,
               layer, b_glu_row):
    d_ssm = d_row.shape[1]
    n_kt = d_ssm // SSM_KTILE
    n_state = ab_re.shape[1]
    full = lambda a: pl.BlockSpec(a.shape, lambda c: (0,) * a.ndim)
    srows = max(nb, V7X_SUBLANES)
    assert srows == nb or srows == 2 * nb
    if srows != nb:
        h0_re = jnp.concatenate([h0_re, h0_re], axis=0)
        h0_im = jnp.concatenate([h0_im, h0_im], axis=0)
    n_chunks = seq // tc
    rb0 = row0 // tc
    u_specs = [pl.BlockSpec((tc, d_ssm), lambda c, b=b: (rb0 + b * n_chunks + c, 0))
               for b in range(nb)]
    a_out, h_re, h_im = pl.pallas_call(
        functools.partial(_ssm_kernel, nb=nb, n_kt=n_kt),
        grid=(n_chunks,),
        in_specs=u_specs + [
            full(h0_re), full(h0_im), full(ab_re), full(ab_im), full(bc), full(cc), full(d_row),
            pl.BlockSpec((None, d_ssm, d_ssm), lambda c: (layer, 0, 0)),
            full(b_glu_row),
        ],
        out_specs=[
            pl.BlockSpec((nb, tc, d_ssm), lambda c: (0, c, 0)),
            pl.BlockSpec((srows, n_state), lambda c: (0, 0)),
            pl.BlockSpec((srows, n_state), lambda c: (0, 0)),
        ],
        out_shape=[
            jax.ShapeDtypeStruct((nb, seq, d_ssm), BF16),
            jax.ShapeDtypeStruct((srows, n_state), F32),
            jax.ShapeDtypeStruct((srows, n_state), F32),
        ],
        scratch_shapes=[pltpu.VMEM((d_ssm // LANES, nb * tc, LANES), F32),
                        pltpu.VMEM((nb * tc, 2 * n_state), F32)],
        compiler_params=_params(("arbitrary",)),
        name="ssm_scan",
    )(*([proj] * nb), h0_re, h0_im, ab_re, ab_im, bc, cc, d_row, w_glu_all, b_glu_row)
    return a_out.reshape(nb * seq, d_ssm), h_re[:nb], h_im[:nb]


def _diff_lambda(lq1_ref, lk1_ref, lq2_ref, lk2_ref, lam_init):
    s1 = jnp.sum(lq1_ref[...] * lk1_ref[...], axis=-1, keepdims=True)
    s2 = jnp.sum(lq2_ref[...] * lk2_ref[...], axis=-1, keepdims=True)
    return jnp.exp(s1) - jnp.exp(s2) + lam_init


def _masked_distance(qpos, kpos):
    dist = jnp.abs(qpos - kpos).astype(F32)
    shift = CHUNK.bit_length() - 1
    assert CHUNK == 1 << shift
    allowed = jnp.right_shift(kpos, shift) <= jnp.right_shift(qpos, shift)
    return jnp.where(allowed, dist, jnp.inf)


def _softmax_step(hm, s, m_scr, l_scr):
    m_old = m_scr[hm]
    m_new = jnp.maximum(m_old, jnp.max(s, axis=-1, keepdims=True))
    alpha = jnp.exp(m_old - m_new)
    p = jnp.exp(s - m_new)
    l_scr[hm] = alpha * l_scr[hm] + jnp.sum(p, axis=-1, keepdims=True)
    m_scr[hm] = m_new
    return p.astype(BF16), alpha


def _attn_tile(q_scr, k_tile, v_tile, distm, m_scr, l_scr, acc_scr):
    scale = HEAD_DIM ** -0.5
    rows = q_scr.shape[0]
    for h in range(N_HEADS):
        bias = (-_alibi_slope(h)) * distm
        ps, alphas = [], []
        for mp in range(2):
            hm = 2 * h + mp
            q = q_scr[:, hm * HEAD_DIM:(hm + 1) * HEAD_DIM]
            s = lax.dot_general(q, k_tile(hm), (((1,), (1,)), ((), ())),
                                preferred_element_type=F32) * scale + bias
            p, alpha = _softmax_step(hm, s, m_scr, l_scr)
            ps.append(p)
            alphas.append(alpha)
        pv = jnp.dot(jnp.concatenate(ps, axis=0), v_tile(h), preferred_element_type=F32)
        for mp in range(2):
            hm = 2 * h + mp
            acc_scr[hm] = alphas[mp] * acc_scr[hm] + pv[mp * rows:(mp + 1) * rows]


def _attn_init(m_scr, l_scr, acc_scr):
    m_scr[...] = jnp.full(m_scr.shape, -jnp.inf, F32)
    l_scr[...] = jnp.zeros(l_scr.shape, F32)
    acc_scr[...] = jnp.zeros(acc_scr.shape, F32)


def _attn_finish(o_ref, lam, gsub_ref, lam_init, m_scr, l_scr, acc_scr):
    for h in range(N_HEADS):
        o = acc_scr[2 * h] / l_scr[2 * h] - lam * (acc_scr[2 * h + 1] / l_scr[2 * h + 1])
        o = _rms(o, gsub_ref[...]) * (1.0 - lam_init)
        o_ref[:, h * V_DIM:(h + 1) * V_DIM] = o.astype(BF16)


def _fold_lanes(x, op):
    out = x[:, :LANES]
    for c in range(1, x.shape[1] // LANES):
        out = op(out, x[:, c * LANES:(c + 1) * LANES])
    return out


def _attn_prompt_kernel(q_ref, k_ref, v_ref, lq1_ref, lk1_ref, lq2_ref, lk2_ref, gsub_ref,
                        o_ref, q_scr, k_scr, v_scr, s_scr, m_scr, l_scr, acc_scr, *, lam_init):
    i = pl.program_id(1)
    tq = q_ref.shape[0]
    scale = HEAD_DIM ** -0.5

    @pl.when(i == 0)
    def _():
        k_scr[...] = k_ref[...].astype(BF16)
        v_scr[...] = v_ref[...].astype(BF16)

    q_scr[...] = q_ref[...].astype(BF16)
    row = lax.broadcasted_iota(jnp.int32, (tq, tq), 0)
    col = lax.broadcasted_iota(jnp.int32, (tq, tq), 1)
    rel = (row - col).astype(F32)
    dist_diag = _masked_distance(i * tq + row, i * tq + col)
    lam = _diff_lambda(lq1_ref, lk1_ref, lq2_ref, lk2_ref, lam_init)

    def scores(h, mp, k0, bias):
        hm = 2 * h + mp
        q = q_scr[:, hm * HEAD_DIM:(hm + 1) * HEAD_DIM]
        k = k_scr[pl.ds(k0, tq), hm * HEAD_DIM:(hm + 1) * HEAD_DIM]
        return lax.dot_general(q, k, (((1,), (1,)), ((), ())),
                               preferred_element_type=F32) * scale + bias

    d0 = pl.multiple_of(i * tq, tq)

    def diag_scores(h):
        slot = h % 2
        bias = (-_alibi_slope(h)) * dist_diag
        for mp in range(2):
            s = scores(h, mp, d0, bias)
            s_scr[slot, mp, :, pl.ds(d0, tq)] = s
            m_scr[slot, mp] = _fold_lanes(s, jnp.maximum)

    def pass1(h, j):
        slot = h % 2
        neg_slope = -_alibi_slope(h)
        k0 = pl.multiple_of(j * tq, tq)
        bias = neg_slope * rel + neg_slope * ((i - j) * tq).astype(F32)
        for mp in range(2):
            s = scores(h, mp, k0, bias)
            s_scr[slot, mp, :, pl.ds(k0, tq)] = s
            m_scr[slot, mp] = jnp.maximum(m_scr[slot, mp], _fold_lanes(s, jnp.maximum))

    def pass2(h, k0, m_row):
        slot = h % 2
        v = v_scr[pl.ds(k0, tq), h * V_DIM:(h + 1) * V_DIM]
        for mp in range(2):
            p = jnp.exp(s_scr[slot, mp, :, pl.ds(k0, tq)] - m_row[mp])
            l_scr[mp] += _fold_lanes(p, jnp.add)
            acc_scr[mp] += jnp.dot(p.astype(BF16), v, preferred_element_type=F32)

    diag_scores(0)
    lax.fori_loop(0, i, lambda j, c: (pass1(0, j), c)[1], 0)
    for h in range(N_HEADS):
        slot = h % 2
        m_row = [jnp.max(m_scr[slot, mp], axis=-1, keepdims=True) for mp in range(2)]
        l_scr[...] = jnp.zeros(l_scr.shape, F32)
        acc_scr[...] = jnp.zeros(acc_scr.shape, F32)
        nxt = h + 1 if h + 1 < N_HEADS else None
        if nxt is not None:
            diag_scores(nxt)

        def both(j, carry, h=h, nxt=nxt, m_row=m_row):
            pass2(h, pl.multiple_of(j * tq, tq), m_row)
            if nxt is not None:
                pass1(nxt, j)
            return carry

        lax.fori_loop(0, i, both, 0)
        pass2(h, d0, m_row)
        l_row = [jnp.sum(l_scr[mp], axis=-1, keepdims=True) for mp in range(2)]
        o = acc_scr[0] / l_row[0] - lam * (acc_scr[1] / l_row[1])
        o = _rms(o, gsub_ref[...]) * (1.0 - lam_init)
        o_ref[:, h * V_DIM:(h + 1) * V_DIM] = o.astype(BF16)


def attn_prompt(proj, seq, batch, lam_rows, gsub_row, lam_init, tq):
    rows = proj.shape[0]
    nq = seq // tq
    lam_spec = pl.BlockSpec((1, HEAD_DIM), lambda b, i: (0, 0))
    return pl.pallas_call(
        functools.partial(_attn_prompt_kernel, lam_init=lam_init),
        grid=(batch, nq),
        in_specs=[
            pl.BlockSpec((tq, D_QK), lambda b, i: (b * nq + i, COL_Q)),
            pl.BlockSpec((seq, D_QK), lambda b, i: (b, COL_K), pipeline_mode=pl.Buffered(1)),
            pl.BlockSpec((seq, D_ATTN), lambda b, i: (b, COL_V), pipeline_mode=pl.Buffered(1)),
            lam_spec, lam_spec, lam_spec, lam_spec,
            pl.BlockSpec((1, V_DIM), lambda b, i: (0, 0)),
        ],
        out_specs=pl.BlockSpec((tq, D_ATTN), lambda b, i: (b * nq + i, 0)),
        out_shape=jax.ShapeDtypeStruct((rows, D_ATTN), BF16),
        scratch_shapes=[
            pltpu.VMEM((tq, D_QK), BF16),
            pltpu.VMEM((seq, D_QK), BF16),
            pltpu.VMEM((seq, D_ATTN), BF16),
            pltpu.VMEM((2, 2, tq, seq), F32),
            pltpu.VMEM((2, 2, tq, LANES), F32),
            pltpu.VMEM((2, tq, LANES), F32),
            pltpu.VMEM((2, tq, V_DIM), F32),
        ],
        compiler_params=_params(("arbitrary", "arbitrary")),
        name="attn_prompt",
    )(proj, proj, proj, *lam_rows, gsub_row)


def _attn_sample_kernel(q_ref, kn_ref, vn_ref, kp_ref, vp_ref, lq1_ref, lk1_ref, lq2_ref, lk2_ref,
                        gsub_ref, o_prev_ref, o_ref, q_scr, m_scr, l_scr, acc_scr, *, lam_init, past):
    del o_prev_ref
    j = pl.program_id(1)
    s_len = q_ref.shape[0]
    tk = kp_ref.shape[0] // CACHE_ROWS

    @pl.when(j == 0)
    def _():
        q_scr[...] = q_ref[...].astype(BF16)
        _attn_init(m_scr, l_scr, acc_scr)

    def v_past(h):
        halves = [vp_ref[pl.ds(c * N_HEADS + h, tk, stride=CACHE_ROWS), :] for c in range(2)]
        return jnp.concatenate(halves, axis=1).astype(BF16)

    qpos = past + lax.broadcasted_iota(jnp.int32, (s_len, tk), 0)
    kpos = j * tk + lax.broadcasted_iota(jnp.int32, (s_len, tk), 1)
    _attn_tile(
        q_scr,
        lambda hm: kp_ref[pl.ds(hm, tk, stride=CACHE_ROWS), :].astype(BF16),
        v_past,
        _masked_distance(qpos, kpos), m_scr, l_scr, acc_scr)

    @pl.when(j == pl.num_programs(1) - 1)
    def _():
        qn = past + lax.broadcasted_iota(jnp.int32, (s_len, s_len), 0)
        kn = past + lax.broadcasted_iota(jnp.int32, (s_len, s_len), 1)
        _attn_tile(
            q_scr,
            lambda hm: kn_ref[:, hm * HEAD_DIM:(hm + 1) * HEAD_DIM].astype(BF16),
            lambda h: vn_ref[:, h * V_DIM:(h + 1) * V_DIM].astype(BF16),
            _masked_distance(qn, kn), m_scr, l_scr, acc_scr)
        lam = _diff_lambda(lq1_ref, lk1_ref, lq2_ref, lk2_ref, lam_init)
        _attn_finish(o_ref, lam, gsub_ref, lam_init, m_scr, l_scr, acc_scr)


def attn_sample(proj, o_prev, row0, s_len, batch, cache_k, cache_v, layer, lam_rows, gsub_row,
                lam_init, tk):
    rows = proj.shape[0]
    depth, _, past = cache_k.shape[:3]
    rb = row0 // s_len
    lam_spec = pl.BlockSpec((1, HEAD_DIM), lambda b, j: (0, 0))
    k_rows = cache_k.reshape(depth, batch, past * CACHE_ROWS, HEAD_DIM)
    v_rows = (cache_v.reshape(depth, batch, past, N_HEADS, 2, HEAD_DIM)
              .transpose(0, 1, 2, 4, 3, 5).reshape(depth, batch, past * CACHE_ROWS, HEAD_DIM))
    cache_spec = pl.BlockSpec((None, None, tk * CACHE_ROWS, HEAD_DIM), lambda b, j: (layer, b, j, 0))
    return pl.pallas_call(
        functools.partial(_attn_sample_kernel, lam_init=lam_init, past=past),
        grid=(batch, past // tk),
        in_specs=[
            pl.BlockSpec((s_len, D_QK), lambda b, j: (rb + b, COL_Q)),
            pl.BlockSpec((s_len, D_QK), lambda b, j: (rb + b, COL_K)),
            pl.BlockSpec((s_len, D_ATTN), lambda b, j: (rb + b, COL_V)),
            cache_spec, cache_spec,
            lam_spec, lam_spec, lam_spec, lam_spec,
            pl.BlockSpec((1, V_DIM), lambda b, j: (0, 0)),
            pl.BlockSpec(memory_space=pl.ANY),
        ],
        out_specs=pl.BlockSpec((s_len, D_ATTN), lambda b, j: (rb + b, 0)),
        out_shape=jax.ShapeDtypeStruct((rows, D_ATTN), BF16),
        input_output_aliases={10: 0},
        scratch_shapes=[
            pltpu.VMEM((s_len, D_QK), BF16),
            pltpu.VMEM((2 * N_HEADS, s_len, 1), F32),
            pltpu.VMEM((2 * N_HEADS, s_len, 1), F32),
            pltpu.VMEM((2 * N_HEADS, s_len, V_DIM), F32),
        ],
        compiler_params=_params(("arbitrary", "arbitrary")),
        name="attn_sample",
    )(proj, proj, proj, k_rows, v_rows, *lam_rows, gsub_row, o_prev)


def _kv_export_kernel(k_ref, v_ref, *refs):
    ko_ref, vo_ref = refs[-2:]
    for h in range(N_HEADS):
        vo_ref[:, h, :] = v_ref[:, h * V_DIM:(h + 1) * V_DIM]
        for mp in range(2):
            hm = 2 * h + mp
            ko_ref[:, h, mp, :] = k_ref[:, hm * HEAD_DIM:(hm + 1) * HEAD_DIM]


def kv_export(proj, row0, seq, batch, depth, layer, k_prev, v_prev, tr):
    rb = row0 // tr
    nr = seq // tr
    in_specs = [
        pl.BlockSpec((tr, D_QK), lambda b, i: (rb + b * nr + i, COL_K)),
        pl.BlockSpec((tr, D_ATTN), lambda b, i: (rb + b * nr + i, COL_V)),
    ]
    args = [proj, proj]
    aliases = {}
    if k_prev is not None:
        in_specs += [pl.BlockSpec(memory_space=pl.ANY), pl.BlockSpec(memory_space=pl.ANY)]
        args += [k_prev, v_prev]
        aliases = {2: 0, 3: 1}
    return pl.pallas_call(
        _kv_export_kernel,
        grid=(batch, seq // tr),
        in_specs=in_specs,
        out_specs=[
            pl.BlockSpec((None, None, tr, N_HEADS, 2, HEAD_DIM), lambda b, i: (layer, b, i, 0, 0, 0)),
            pl.BlockSpec((None, None, tr, N_HEADS, V_DIM), lambda b, i: (layer, b, i, 0, 0)),
        ],
        out_shape=[
            jax.ShapeDtypeStruct((depth, batch, seq, N_HEADS, 2, HEAD_DIM), F32),
            jax.ShapeDtypeStruct((depth, batch, seq, N_HEADS, V_DIM), F32),
        ],
        input_output_aliases=aliases,
        compiler_params=_params(("arbitrary", "arbitrary")),
        name="kv_export",
    )(*args)


def _merge_kernel(ap_ref, as_ref, o_ref, wa_ref, wo_ref, ga_ref, go_ref, out_ref, *, n_prompt):
    a = jnp.where(pl.program_id(0) < n_prompt, ap_ref[...], as_ref[...])
    ya = jnp.dot(a, wa_ref[...], preferred_element_type=F32)
    yo = jnp.dot(o_ref[...], wo_ref[...], preferred_element_type=F32)
    merged = ga_ref[...].astype(F32) * ya + go_ref[...].astype(F32) * yo
    out_ref[...] = merged.astype(BF16)


def gated_merge(a_prompt, a_sample, o_out, w_ssm_all, w_attn_all, gates, layer, tm, tn):
    m, d_br = o_out.shape
    d_model = w_ssm_all.shape[2]
    nj = d_model // tn
    n_prompt = a_prompt.shape[0] // tm
    assert a_prompt.shape[0] % tm == 0 and a_sample.shape[0] % tm == 0
    return pl.pallas_call(
        functools.partial(_merge_kernel, n_prompt=n_prompt),
        grid=(m // tm, nj),
        in_specs=[
            pl.BlockSpec((tm, d_br), lambda i, j: (jnp.minimum(i, n_prompt - 1), 0)),
            pl.BlockSpec((tm, d_br), lambda i, j: (jnp.maximum(i - n_prompt, 0), 0)),
            pl.BlockSpec((tm, d_br), lambda i, j: (i, 0)),
            pl.BlockSpec((None, d_br, tn), lambda i, j: (layer, 0, j)),
            pl.BlockSpec((None, d_br, tn), lambda i, j: (layer, 0, j)),
            pl.BlockSpec((tm, tn), lambda i, j: (i, j)),
            pl.BlockSpec((tm, tn), lambda i, j: (i, nj + j)),
        ],
        out_specs=pl.BlockSpec((tm, tn), lambda i, j: (i, j)),
        out_shape=jax.ShapeDtypeStruct((m, d_model), BF16),
        compiler_params=_params(("parallel", "arbitrary")),
        name="gated_merge",
    )(a_prompt, a_sample, o_out, w_ssm_all, w_attn_all, gates, gates)


def _out_proj_kernel(a_ref, w_ref, x_ref, g_ref, o_ref, *, tn):
    n = o_ref.shape[1]
    a = a_ref[...]
    ssq = jnp.zeros((a.shape[0], 1), F32)
    for c in range(n // tn):
        y = jnp.dot(a, w_ref[:, c * tn:(c + 1) * tn], preferred_element_type=F32)
        ssq = ssq + jnp.sum(y * y, axis=-1, keepdims=True)
        o_ref[:, c * tn:(c + 1) * tn] = y
    inv = lax.rsqrt(ssq / n + EPS)
    o_ref[...] = x_ref[...] + o_ref[...] * inv * g_ref[...]


def out_proj_residual(merged, w_all, x, g_row, layer, tm, tn):
    m, d = x.shape
    k = merged.shape[1]
    return pl.pallas_call(
        functools.partial(_out_proj_kernel, tn=tn),
        grid=(m // tm,),
        in_specs=[
            pl.BlockSpec((tm, k), lambda i: (i, 0)),
            pl.BlockSpec((None, k, d), lambda i: (layer, 0, 0)),
            pl.BlockSpec((tm, d), lambda i: (i, 0)),
            pl.BlockSpec((1, d), lambda i: (0, 0)),
        ],
        out_specs=pl.BlockSpec((tm, d), lambda i: (i, 0)),
        out_shape=jax.ShapeDtypeStruct((m, d), F32),
        compiler_params=_params(("parallel",)),
        name="out_proj_residual",
    )(merged, w_all, x, g_row)


def _ffn_kernel(x_ref, gpre_ref, wg_ref, wu_ref, wd_ref, gpost_ref, o_ref, h_scr):
    j = pl.program_id(1)

    @pl.when(j == 0)
    def _():
        h_scr[...] = _rms(x_ref[...], gpre_ref[...]).astype(BF16)
        o_ref[...] = jnp.zeros(o_ref.shape, F32)

    h = h_scr[...]
    gate = jnp.dot(h, wg_ref[...], preferred_element_type=F32)
    up = jnp.dot(h, wu_ref[...], preferred_element_type=F32)
    act = (gate * _sigmoid(gate) * up).astype(BF16)
    o_ref[...] += jnp.dot(act, wd_ref[...], preferred_element_type=F32)

    @pl.when(j == pl.num_programs(1) - 1)
    def _():
        o_ref[...] = x_ref[...] + _rms(o_ref[...], gpost_ref[...])


def ffn_residual(x, gpre_row, wg_all, wu_all, wd_all, gpost_row, layer, tm, tf):
    m, d = x.shape
    f = wg_all.shape[2]
    return pl.pallas_call(
        _ffn_kernel,
        grid=(m // tm, f // tf),
        in_specs=[
            pl.BlockSpec((tm, d), lambda i, j: (i, 0)),
            pl.BlockSpec((1, d), lambda i, j: (0, 0)),
            pl.BlockSpec((None, d, tf), lambda i, j: (layer, 0, j)),
            pl.BlockSpec((None, d, tf), lambda i, j: (layer, 0, j)),
            pl.BlockSpec((None, tf, d), lambda i, j: (layer, j, 0)),
            pl.BlockSpec((1, d), lambda i, j: (0, 0)),
        ],
        out_specs=pl.BlockSpec((tm, d), lambda i, j: (i, 0)),
        out_shape=jax.ShapeDtypeStruct((m, d), F32),
        scratch_shapes=[pltpu.VMEM((tm, d), BF16)],
        compiler_params=_params(("parallel", "arbitrary")),
        name="swiglu_residual",
    )(x, gpre_row, wg_all, wu_all, wd_all, gpost_row)


def _tile(n, want):
    t = min(n, want)
    assert n % t == 0, (n, want)
    return t


def kernel(x_prompt, x_sample, cache_k, cache_v, state_ssm_re, state_ssm_im, g_pre_mix, w_in, b_gate, ssm_a_re, ssm_a_im, ssm_log_dt, ssm_b_re, ssm_b_im, ssm_c_re, ssm_c_im, ssm_d, w_glu, b_glu, lam_q1, lam_k1, lam_q2, lam_k2, g_sub, w_br_ssm, w_br_attn, w_out, g_post_mix, g_pre_ffn, w_ffn_gate, w_ffn_up, w_ffn_down, g_post_ffn):
    batch, seq, d_model = x_prompt.shape
    dec_batch, dec_seq, _ = x_sample.shape
    depth = w_in.shape[0]
    past = cache_k.shape[2]
    n_groups, state_dim = ssm_a_re.shape[1:]
    n_state = n_groups * state_dim
    assert ssm_d.shape[1] == D_SSM and w_in.shape[2] == COL_GATE * D_QK + N_BRANCH * d_model
    assert w_in.shape[2] % D_QK == 0
    rows_p = seq * batch
    rows_s = dec_seq * dec_batch

    x = jnp.concatenate([x_prompt.reshape(rows_p, d_model),
                         x_sample.reshape(rows_s, d_model)], axis=0)

    w_in_b = w_in.astype(BF16)
    w_glu_b = w_glu.astype(BF16)
    w_br_ssm_b = w_br_ssm.astype(BF16)
    w_br_attn_b = w_br_attn.astype(BF16)
    w_out_b = w_out.astype(BF16)
    w_gate_b = w_ffn_gate.astype(BF16)
    w_up_b = w_ffn_up.astype(BF16)
    w_down_b = w_ffn_down.astype(BF16)

    tm = _tile(rows_s, 1024)
    assert rows_p % tm == 0
    ssm_rows = 256
    zeros_state = jnp.zeros((batch, n_state), F32)

    hr_p, hi_p, hr_s, hi_s = [], [], [], []
    k_p = v_p = k_s = v_s = None
    for l in range(depth):
        lam_init = _lambda_init(l)
        proj, gates = norm_matmul(x, g_pre_mix[l][None], w_in_b,
                                  b_gate[l].reshape(1, N_BRANCH * d_model), l, tm,
                                  _tile(N_BRANCH * d_model, 1024))

        ab_re, ab_im, bc, cc = ssm_prep(ssm_a_re[l], ssm_a_im[l], ssm_log_dt[l], ssm_b_re[l],
                                        ssm_b_im[l], ssm_c_re[l], ssm_c_im[l])
        d_row = ssm_d[l][None]
        bg_row = b_glu[l][None]
        a_p, hrp, hip = ssm_branch(proj, 0, seq, batch, ssm_rows // batch, zeros_state, zeros_state,
                                   ab_re, ab_im, bc, cc, d_row, w_glu_b, l, bg_row)
        a_s, hrs, his = ssm_branch(proj, rows_p, dec_seq, dec_batch, ssm_rows // dec_batch,
                                   state_ssm_re[l].reshape(dec_batch, n_state),
                                   state_ssm_im[l].reshape(dec_batch, n_state),
                                   ab_re, ab_im, bc, cc, d_row, w_glu_b, l, bg_row)

        lam_rows = (lam_q1[l][None], lam_k1[l][None], lam_q2[l][None], lam_k2[l][None])
        gsub_row = g_sub[l][None]
        o_out = attn_prompt(proj, seq, batch, lam_rows, gsub_row, lam_init, _tile(seq, 512))
        o_out = attn_sample(proj, o_out, rows_p, dec_seq, dec_batch, cache_k, cache_v, l, lam_rows,
                            gsub_row, lam_init, _tile(past, 2048))
        k_p, v_p = kv_export(proj, 0, seq, batch, depth, l, k_p, v_p, _tile(seq, 1024))
        k_s, v_s = kv_export(proj, rows_p, dec_seq, dec_batch, depth, l, k_s, v_s, dec_seq)

        merged = gated_merge(a_p, a_s, o_out, w_br_ssm_b, w_br_attn_b, gates, l, tm,
                             _tile(d_model, 1024))
        x = out_proj_residual(merged, w_out_b, x, g_post_mix[l][None], l, _tile(rows_s, 512),
                              _tile(d_model, 512))
        x = ffn_residual(x, g_pre_ffn[l][None], w_gate_b, w_up_b, w_down_b, g_post_ffn[l][None],
                         l, _tile(rows_s, 512), _tile(w_ffn_gate.shape[2], 512))

        hr_p.append(hrp.reshape(batch, n_groups, state_dim))
        hi_p.append(hip.reshape(batch, n_groups, state_dim))
        hr_s.append(hrs.reshape(dec_batch, n_groups, state_dim))
        hi_s.append(his.reshape(dec_batch, n_groups, state_dim))

    y_p = x[:rows_p].reshape(batch, seq, d_model)
    y_s = x[rows_p:].reshape(dec_batch, dec_seq, d_model)
    return (y_p, y_s, k_p, v_p, jnp.stack(hr_p), jnp.stack(hi_p),
            k_s, v_s, jnp.stack(hr_s), jnp.stack(hi_s))
```
